```python
import jax, jax.numpy as jnp
from jax import lax
import numpy as np

D_MODEL = 1024
BATCH = 8
SEQ = 4096
DEPTH = 1
DEC_BATCH = 128
DEC_SEQ = 8
PAST_LEN = 16384
PAGE_SIZE = 128

POOL_WINDOWS = (2, 4, 8, 16)
POOL_GROUPS = len(POOL_WINDOWS)
POOL_GROUP_DIM = D_MODEL // 8
POOL_DIM = POOL_GROUPS * POOL_GROUP_DIM
POOL_STATE = max(POOL_WINDOWS) - 1
N_HEADS = 8
QK_NOPE = 64
QK_ROPE = 32
QK_HEAD = QK_NOPE + QK_ROPE
V_HEAD = 64
Q_LORA = 384
KV_LORA = 256
ATTN_DIM = N_HEADS * V_HEAD
ROPE_THETA = 10000.0
SCALE = QK_HEAD ** -0.5
Q_BLOCK = 128
NEG = -1e30
N_BRANCH = 2
OFF_KV = Q_LORA
OFF_KR = OFF_KV + KV_LORA
OFF_POOL = OFF_KR + QK_ROPE
OFF_GATE = OFF_POOL + POOL_DIM
IN_COLS = OFF_GATE + N_BRANCH * D_MODEL
PEER_HEADS = 8
PEER_NKEYS = 128
PEER_EXPERTS = PEER_NKEYS * PEER_NKEYS
PEER_QDIM = 256
PEER_HALF = PEER_QDIM // 2
PEER_TOPK = 16
PEER_BLOCK = 128
EPS = 1e-6

kernel_name = 'pool_mla_peer_hybrid_step'


def rmsnorm(x, w):
    x32 = x.astype(jnp.float32)
    y = x32 * lax.rsqrt(jnp.mean(x32 * x32, axis=-1, keepdims=True) + EPS)
    return (y * w.astype(jnp.float32)).astype(x.dtype)


def rope(x, pos):
    half = QK_ROPE // 2
    inv_freq = ROPE_THETA ** (-jnp.arange(half, dtype=jnp.float32) / half)
    ang = pos.astype(jnp.float32)[:, None] * inv_freq[None, :]
    cos = jnp.cos(ang)[:, None, :]
    sin = jnp.sin(ang)[:, None, :]
    x32 = x.astype(jnp.float32)
    x1, x2 = x32[..., :half], x32[..., half:]
    return jnp.concatenate([x1 * cos - x2 * sin, x1 * sin + x2 * cos], axis=-1).astype(x.dtype)


def mixer_projections(xn, pos, p):
    proj = jnp.einsum('btd,de->bte', xn, p['w_in'])
    q_lat = proj[..., :OFF_KV]
    kv_lat = proj[..., OFF_KV:OFF_KR]
    k_rope = proj[..., OFF_KR:OFF_POOL]
    u_pool = proj[..., OFF_POOL:OFF_GATE]
    gates = proj[..., OFF_GATE:]
    q = jnp.einsum('btr,rhd->bthd', rmsnorm(q_lat, p['q_lat_norm_w']), p['w_q_up'])
    q = jnp.concatenate([q[..., :QK_NOPE], rope(q[..., QK_NOPE:], pos)], axis=-1)
    q = rmsnorm(q, p['q_norm_w'])
    c = rmsnorm(kv_lat, p['kv_lat_norm_w'])
    kr = rope(k_rope[:, :, None, :], pos)[:, :, 0, :]
    return q, c, kr, u_pool, gates


def mla_keys(c, kr, p):
    k_nope = jnp.einsum('btc,chd->bthd', c, p['w_uk'])
    kr_b = jnp.broadcast_to(kr[:, :, None, :], k_nope.shape[:-1] + (QK_ROPE,))
    return rmsnorm(jnp.concatenate([k_nope, kr_b], axis=-1), p['k_norm_w'])


def prompt_attention(q, c, kr, p):
    B, S = q.shape[:2]
    k = mla_keys(c, kr, p)
    v = jnp.einsum('btc,chd->bthd', c, p['w_uv'])
    key_pos = jnp.arange(S)

    def block(i):
        q_blk = lax.dynamic_slice_in_dim(q, i * Q_BLOCK, Q_BLOCK, axis=1)
        s = jnp.einsum('bqhd,bthd->bhqt', q_blk, k).astype(jnp.float32) * SCALE
        q_pos = i * Q_BLOCK + jnp.arange(Q_BLOCK)
        s = jnp.where(key_pos[None, :] <= q_pos[:, None], s, NEG)
        pr = jax.nn.softmax(s, axis=-1).astype(v.dtype)
        return jnp.einsum('bhqt,bthd->bqhd', pr, v)

    o = lax.map(block, jnp.arange(S // Q_BLOCK))
    return o.transpose(1, 0, 2, 3, 4).reshape(B, S, ATTN_DIM)


def online_update(m, l, acc, s, c):
    m_new = jnp.maximum(m, jnp.max(s, axis=-1))
    corr = jnp.exp(m - m_new)
    pr = jnp.exp(s - m_new[..., None])
    l_new = l * corr + jnp.sum(pr, axis=-1)
    acc_new = acc * corr[..., None] + jnp.einsum('bhqt,btc->bhqc', pr, c.astype(jnp.float32))
    return m_new, l_new, acc_new


def sample_attention(q, c_new, kr_new, ckv_pages, kr_pages, page_table, p):
    DB, Q = q.shape[:2]

    def page_step(carry, phys):
        m, l, acc = carry
        c_pg = ckv_pages[phys]
        k_pg = mla_keys(c_pg, kr_pages[phys], p)
        s = jnp.einsum('bqhd,bthd->bhqt', q, k_pg).astype(jnp.float32) * SCALE
        return online_update(m, l, acc, s, c_pg), None

    init = (jnp.full((DB, N_HEADS, Q), NEG, jnp.float32),
            jnp.zeros((DB, N_HEADS, Q), jnp.float32),
            jnp.zeros((DB, N_HEADS, Q, KV_LORA), jnp.float32))
    (m, l, acc), _ = lax.scan(page_step, init, page_table.T)
    k_new = mla_keys(c_new, kr_new, p)
    s = jnp.einsum('bqhd,bthd->bhqt', q, k_new).astype(jnp.float32) * SCALE
    s = jnp.where(jnp.tril(jnp.ones((Q, Q), bool)), s, NEG)
    m, l, acc = online_update(m, l, acc, s, c_new)
    lat = (acc / l[..., None]).astype(c_new.dtype)
    o = jnp.einsum('bhqc,chd->bqhd', lat, p['w_uv'])
    return o.reshape(DB, Q, ATTN_DIM)


def pool_mix(u_full, n_new, p):
    B, L = u_full.shape[:2]
    u32 = u_full.astype(jnp.float32)
    cs = jnp.concatenate([jnp.zeros_like(u32[:, :1]), jnp.cumsum(u32, axis=1)], axis=1)
    rows = np.arange(L - n_new, L)
    outs = []
    for g, w in enumerate(POOL_WINDOWS):
        lo_c, hi_c = g * POOL_GROUP_DIM, (g + 1) * POOL_GROUP_DIM
        hi = rows + 1
        lo = np.maximum(hi - w, 0)
        cnt = (hi - lo).astype(np.float32)
        mean = (cs[:, hi, lo_c:hi_c] - cs[:, lo, lo_c:hi_c]) / cnt[None, :, None]
        outs.append(mean - u32[:, rows, lo_c:hi_c])
    z = jnp.stack(outs, axis=2).astype(u_full.dtype)
    z = jnp.einsum('bngc,gcd->bngd', z, p['pool_lin_w']).reshape(B, n_new, POOL_DIM)
    return z * p['pool_scale']


def peer(xn, p):
    B, T, D = xn.shape
    n_tok = B * T
    xt = xn.reshape(n_tok, D)
    q = jnp.einsum('nd,de->ne', xt, p['peer_w_q']).reshape(n_tok, PEER_HEADS, 2, PEER_HALF)
    s = jnp.einsum('nhpd,hpkd->nhpk', q, p['peer_keys']).astype(jnp.float32)
    s_top, i_top = lax.top_k(s, PEER_TOPK)
    cand = (s_top[:, :, 0, :, None] + s_top[:, :, 1, None, :]).reshape(n_tok, PEER_HEADS, PEER_TOPK * PEER_TOPK)
    cand_id = (i_top[:, :, 0, :, None] * PEER_NKEYS + i_top[:, :, 1, None, :]).reshape(n_tok, PEER_HEADS, PEER_TOPK * PEER_TOPK)
    best, pos = lax.top_k(cand, PEER_TOPK)
    expert = jnp.take_along_axis(cand_id, pos, axis=-1)
    gate = jax.nn.softmax(best, axis=-1).astype(xn.dtype)
    n_pad = (-n_tok) % PEER_BLOCK
    nb = (n_tok + n_pad) // PEER_BLOCK
    xb = jnp.pad(xt, ((0, n_pad), (0, 0))).reshape(nb, PEER_BLOCK, D)
    eb = jnp.pad(expert.reshape(n_tok, -1), ((0, n_pad), (0, 0))).reshape(nb, PEER_BLOCK, PEER_HEADS * PEER_TOPK)
    gb = jnp.pad(gate.reshape(n_tok, -1), ((0, n_pad), (0, 0))).reshape(nb, PEER_BLOCK, PEER_HEADS * PEER_TOPK)

    def block(args):
        xk, ek, gk = args
        u = p['peer_u'][ek]
        v = p['peer_v'][ek]
        a = jax.nn.gelu(jnp.einsum('td,ted->te', xk, u))
        return jnp.einsum('te,ted->td', gk * a, v)

    out = lax.map(block, (xb, eb, gb)).reshape(nb * PEER_BLOCK, D)[:n_tok]
    return out.reshape(B, T, D)


def merge_and_channel_mix(x, attn, pool, gates, p):
    br_attn = jnp.einsum('bte,ed->btd', attn, p['w_attn_br'])
    br_pool = jnp.einsum('btc,cd->btd', pool, p['w_pool_br'])
    mixed = jax.nn.sigmoid(gates[..., :D_MODEL]) * br_attn + jax.nn.sigmoid(gates[..., D_MODEL:]) * br_pool
    h = x + jnp.einsum('btd,de->bte', mixed, p['w_out'])
    return h + peer(rmsnorm(h, p['norm2_w']), p)


def setup_inputs(seed: int = 0) -> dict:
    key = jax.random.key(seed)
    ks = jax.random.split(key, 32)
    n_pages = PAST_LEN // PAGE_SIZE
    n_used = DEC_BATCH * n_pages
    n_phys = n_used + n_used // 4
    nrm = lambda k, shape, s: jax.random.normal(k, shape, jnp.float32) * s
    gain = lambda k, n: 1.0 + nrm(k, (DEPTH, n), 0.05)
    page_table = jax.random.permutation(ks[5], n_phys)[:n_used].reshape(DEC_BATCH, n_pages).astype(jnp.int32)
    return {
        'x_prompt': nrm(ks[0], (BATCH, SEQ, D_MODEL), 1.0),
        'x_sample': nrm(ks[1], (DEC_BATCH, DEC_SEQ, D_MODEL), 1.0),
        'cache_ckv': nrm(ks[2], (DEPTH, n_phys, PAGE_SIZE, KV_LORA), 1.0),
        'cache_krope': nrm(ks[3], (DEPTH, n_phys, PAGE_SIZE, QK_ROPE), 1.0),
        'state_pool': nrm(ks[4], (DEPTH, DEC_BATCH, POOL_STATE, POOL_DIM), 1.0),
        'page_table': page_table,
        'norm1_w': gain(ks[6], D_MODEL),
        'w_in': nrm(ks[7], (DEPTH, D_MODEL, IN_COLS), D_MODEL ** -0.5),
        'q_lat_norm_w': gain(ks[8], Q_LORA),
        'w_q_up': nrm(ks[9], (DEPTH, Q_LORA, N_HEADS, QK_HEAD), Q_LORA ** -0.5),
        'kv_lat_norm_w': gain(ks[10], KV_LORA),
        'w_uk': nrm(ks[11], (DEPTH, KV_LORA, N_HEADS, QK_NOPE), KV_LORA ** -0.5),
        'w_uv': nrm(ks[12], (DEPTH, KV_LORA, N_HEADS, V_HEAD), KV_LORA ** -0.5),
        'q_norm_w': gain(ks[13], QK_HEAD),
        'k_norm_w': gain(ks[14], QK_HEAD),
        'w_attn_br': nrm(ks[15], (DEPTH, ATTN_DIM, D_MODEL), ATTN_DIM ** -0.5),
        'pool_lin_w': nrm(ks[16], (DEPTH, POOL_GROUPS, POOL_GROUP_DIM, POOL_GROUP_DIM), POOL_GROUP_DIM ** -0.5),
        'pool_scale': 1.0 + nrm(ks[17], (DEPTH, POOL_DIM), 0.1),
        'w_pool_br': nrm(ks[18], (DEPTH, POOL_DIM, D_MODEL), POOL_DIM ** -0.5),
        'w_out': nrm(ks[19], (DEPTH, D_MODEL, D_MODEL), D_MODEL ** -0.5),
        'norm2_w': gain(ks[20], D_MODEL),
        'peer_w_q': nrm(ks[21], (DEPTH, D_MODEL, PEER_HEADS * PEER_QDIM), D_MODEL ** -0.5),
        'peer_keys': nrm(ks[22], (DEPTH, PEER_HEADS, 2, PEER_NKEYS, PEER_HALF), PEER_HALF ** -0.5),
        'peer_u': nrm(ks[23], (DEPTH, PEER_EXPERTS, D_MODEL), D_MODEL ** -0.5),
        'peer_v': nrm(ks[24], (DEPTH, PEER_EXPERTS, D_MODEL), PEER_HEADS ** -0.5),
    }


def reference(x_prompt, x_sample, cache_ckv, cache_krope, state_pool, page_table,
              norm1_w, w_in, q_lat_norm_w, w_q_up, kv_lat_norm_w, w_uk, w_uv, q_norm_w, k_norm_w,
              w_attn_br, pool_lin_w, pool_scale, w_pool_br, w_out, norm2_w,
              peer_w_q, peer_keys, peer_u, peer_v):
    B, S = x_prompt.shape[:2]
    DB, Q = x_sample.shape[:2]
    pos_p = jnp.arange(S)
    pos_s = PAST_LEN + jnp.arange(Q)
    xp, xs = x_prompt, x_sample
    ckv_p, kr_p, pool_p, ckv_s, kr_s, pool_s = [], [], [], [], [], []
    for l in range(DEPTH):
        p = dict(norm1_w=norm1_w[l], w_in=w_in[l], q_lat_norm_w=q_lat_norm_w[l], w_q_up=w_q_up[l],
                 kv_lat_norm_w=kv_lat_norm_w[l], w_uk=w_uk[l], w_uv=w_uv[l], q_norm_w=q_norm_w[l],
                 k_norm_w=k_norm_w[l], w_attn_br=w_attn_br[l], pool_lin_w=pool_lin_w[l],
                 pool_scale=pool_scale[l], w_pool_br=w_pool_br[l], w_out=w_out[l], norm2_w=norm2_w[l],
                 peer_w_q=peer_w_q[l], peer_keys=peer_keys[l], peer_u=peer_u[l], peer_v=peer_v[l])
        q, c, kr, u_pool, gates = mixer_projections(rmsnorm(xp, p['norm1_w']), pos_p, p)
        attn = prompt_attention(q, c, kr, p)
        pool = pool_mix(u_pool, S, p)
        xp = merge_and_channel_mix(xp, attn, pool, gates, p)
        ckv_p.append(c)
        kr_p.append(kr)
        pool_p.append(u_pool[:, S - POOL_STATE:])
        q, c, kr, u_pool, gates = mixer_projections(rmsnorm(xs, p['norm1_w']), pos_s, p)
        attn = sample_attention(q, c, kr, cache_ckv[l], cache_krope[l], page_table, p)
        u_full = jnp.concatenate([state_pool[l].astype(u_pool.dtype), u_pool], axis=1)
        pool = pool_mix(u_full, Q, p)
        xs = merge_and_channel_mix(xs, attn, pool, gates, p)
        ckv_s.append(c)
        kr_s.append(kr)
        pool_s.append(u_full[:, -POOL_STATE:])
    return (xp, xs, jnp.stack(ckv_p), jnp.stack(kr_p), jnp.stack(pool_p),
            jnp.stack(ckv_s), jnp.stack(kr_s), jnp.stack(pool_s))
```

```python
import functools
import math

import jax
import jax.numpy as jnp
from jax import lax
from jax.experimental import pallas as pl
from jax.experimental.pallas import tpu as pltpu

F32 = jnp.float32
BF16 = jnp.bfloat16

EPS = 1e-6
ROPE_THETA = 10000.0
POOL_WINDOWS = (2, 4, 8, 16)
PEER_TOPK = 16
NEG = -1e30
LANES = 128
HEAD_PAD = 128
VMEM_LIMIT = 56 * 1024 * 1024


def _nt(a, b):
    return lax.dot_general(a, b, (((1,), (1,)), ((), ())), preferred_element_type=F32)


def _mm(a, b):
    return jnp.dot(a, b, preferred_element_type=F32)


def _rms(x, w):
    return x * lax.rsqrt(jnp.mean(x * x, axis=-1, keepdims=True) + EPS) * w


def _params(**kw):
    return pltpu.CompilerParams(vmem_limit_bytes=VMEM_LIMIT, **kw)


def _full(shape):
    n = len(shape)
    return pl.BlockSpec(shape, lambda *_: (0,) * n)


def _head_norm(blk, ta, tb, w, d_head):
    rot = pltpu.roll(blk, HEAD_PAD - 32, axis=1)
    hq = blk * ta + rot * tb
    ssq = jnp.sum(hq * hq, axis=-1, keepdims=True)
    return hq * lax.rsqrt(ssq * (1.0 / d_head) + EPS) * w


def _proj_body(x_ref, ta_ref, tb_ref, n1_ref, win_ref, qlw_ref, wq_ref, kvw_ref, wuk_ref, wuv_ref,
               qnw_ref, knw_ref, q_ref, k_ref, v_ref, c_ref, kr_ref, u_ref, g_ref,
               *, q_lora, kv_lora, pool_dim, n_heads, d_head):
    x = x_ref[...]
    xn = _rms(x, n1_ref[...]).astype(BF16)
    proj = _mm(xn, win_ref[...])
    o1 = q_lora
    o2 = o1 + kv_lora
    o3 = o2 + HEAD_PAD
    o4 = o3 + pool_dim
    ta = ta_ref[...]
    tb = tb_ref[...]

    ql = _rms(proj[:, :o1], qlw_ref[...]).astype(BF16)
    qu = _mm(ql, wq_ref[...])
    c = _rms(proj[:, o1:o2], kvw_ref[...])
    c_ref[...] = c
    cb = c.astype(BF16)
    krb = proj[:, o2:o3]
    krp = krb * ta + pltpu.roll(krb, HEAD_PAD - 32, axis=1) * tb
    kr_ref[...] = krp
    ku = _mm(cb, wuk_ref[...])
    v_ref[...] = _mm(cb, wuv_ref[...]).astype(BF16)
    qnw = qnw_ref[...]
    knw = knw_ref[...]
    for h in range(n_heads):
        sl = slice(h * HEAD_PAD, (h + 1) * HEAD_PAD)
        q_ref[:, sl] = _head_norm(qu[:, sl], ta, tb, qnw, d_head).astype(BF16)
        kh = ku[:, sl] + krp
        ssq = jnp.sum(kh * kh, axis=-1, keepdims=True)
        k_ref[:, sl] = (kh * lax.rsqrt(ssq * (1.0 / d_head) + EPS) * knw).astype(BF16)
    u_ref[...] = proj[:, o3:o4]
    g_ref[...] = jax.nn.sigmoid(proj[:, o4:]).astype(BF16)


def _proj(x2d, ta, tb, w, dims, tm):
    n, d = x2d.shape
    nt = n // tm
    tab_blocks = ta.shape[0] // tm
    row = lambda i: (i, 0)
    tab = lambda i: (i % tab_blocks, 0)
    hp = dims["n_heads"] * HEAD_PAD
    body = functools.partial(_proj_body, q_lora=dims["q_lora"], kv_lora=dims["kv_lora"],
                             pool_dim=dims["pool_dim"], n_heads=dims["n_heads"], d_head=dims["d_head"])
    consts = [w["n1"], w["win"], w["qlw"], w["wq"], w["kvw"], w["wuk"], w["wuv"], w["qnw"], w["knw"]]
    return pl.pallas_call(
        body,
        grid=(nt,),
        in_specs=[pl.BlockSpec((tm, d), row), pl.BlockSpec((tm, HEAD_PAD), tab), pl.BlockSpec((tm, HEAD_PAD), tab)]
                 + [_full(a.shape) for a in consts],
        out_specs=[pl.BlockSpec((tm, hp), row), pl.BlockSpec((tm, hp), row), pl.BlockSpec((tm, hp), row),
                   pl.BlockSpec((tm, dims["kv_lora"]), row), pl.BlockSpec((tm, HEAD_PAD), row),
                   pl.BlockSpec((tm, dims["pool_dim"]), row), pl.BlockSpec((tm, 2 * d), row)],
        out_shape=[jax.ShapeDtypeStruct((n, hp), BF16), jax.ShapeDtypeStruct((n, hp), BF16),
                   jax.ShapeDtypeStruct((n, hp), BF16), jax.ShapeDtypeStruct((n, dims["kv_lora"]), F32),
                   jax.ShapeDtypeStruct((n, HEAD_PAD), F32), jax.ShapeDtypeStruct((n, dims["pool_dim"]), F32),
                   jax.ShapeDtypeStruct((n, 2 * d), BF16)],
        compiler_params=_params(dimension_semantics=("parallel",)),
        name="proj",
    )(x2d, ta, tb, *consts)


def _window_sums(ext):
    s2 = ext + pltpu.roll(ext, 1, axis=0)
    s4 = s2 + pltpu.roll(s2, 2, axis=0)
    s8 = s4 + pltpu.roll(s4, 4, axis=0)
    s16 = s8 + pltpu.roll(s8, 8, axis=0)
    return (s2, s4, s8, s16)


def _pool_tail(z_groups, g2, lin_ref, scale_ref, wbr_ref, out_ref):
    zl = [_mm(z.astype(BF16), lin_ref[g]) for g, z in enumerate(z_groups)]
    pool = jnp.concatenate(zl, axis=-1) * scale_ref[...]
    br = _mm(pool.astype(BF16), wbr_ref[...])
    out_ref[...] = (g2.astype(F32) * br).astype(BF16)


def _pool_prompt_body(u_ref, halo_ref, g_ref, lin_ref, scale_ref, wbr_ref, out_ref, *, tm, gd):
    j = pl.program_id(1)
    u = u_ref[...]
    prev = jnp.where(j == 0, 0.0, halo_ref[...])
    sums = _window_sums(jnp.concatenate([prev, u], axis=0))
    pos = j * tm + lax.broadcasted_iota(jnp.int32, (tm, gd), 0)
    zs = []
    for g, wdw in enumerate(POOL_WINDOWS):
        sl = slice(g * gd, (g + 1) * gd)
        cnt = jnp.minimum(pos + 1, wdw).astype(F32)
        zs.append(sums[g][16:, sl] / cnt - u[:, sl])
    _pool_tail(zs, g_ref[...], lin_ref, scale_ref, wbr_ref, out_ref)


def _pool_prompt(u2d, gates, w, batch, seq, tm):
    n, pd = u2d.shape
    d = gates.shape[1] // 2
    ns = seq // tm
    gd = pd // len(POOL_WINDOWS)
    hb = tm // 16
    body = functools.partial(_pool_prompt_body, tm=tm, gd=gd)
    return pl.pallas_call(
        body,
        grid=(batch, ns),
        in_specs=[pl.BlockSpec((tm, pd), lambda b, j: (b * ns + j, 0)),
                  pl.BlockSpec((16, pd), lambda b, j: (jnp.maximum((b * ns + j) * hb - 1, 0), 0)),
                  pl.BlockSpec((tm, d), lambda b, j: (b * ns + j, 1)),
                  _full(w["pool_lin"].shape), _full(w["pool_scale"].shape), _full(w["wpbr"].shape)],
        out_specs=pl.BlockSpec((tm, d), lambda b, j: (b * ns + j, 0)),
        out_shape=jax.ShapeDtypeStruct((n, d), BF16),
        compiler_params=_params(dimension_semantics=("parallel", "parallel")),
        name="pool_prompt",
    )(u2d, u2d, gates, w["pool_lin"], w["pool_scale"], w["wpbr"])


def _pool_sample_body(ext_ref, g_ref, lin_ref, scale_ref, wbr_ref, out_ref, *, nb, rows, n_new, gd):
    ext = ext_ref[...]
    sums = _window_sums(ext)
    pick = lambda a: a.reshape(nb, rows, a.shape[-1])[:, rows - n_new:, :].reshape(nb * n_new, a.shape[-1])
    zs = []
    for g, wdw in enumerate(POOL_WINDOWS):
        sl = slice(g * gd, (g + 1) * gd)
        zs.append(pick(sums[g][:, sl]) / float(wdw) - pick(ext[:, sl]))
    _pool_tail(zs, g_ref[...], lin_ref, scale_ref, wbr_ref, out_ref)


def _pool_sample(ext2d, gates, w, db, rows, n_new):
    pd = ext2d.shape[1]
    d = gates.shape[1] // 2
    gd = pd // len(POOL_WINDOWS)
    nb = _tile(db, 16)
    body = functools.partial(_pool_sample_body, nb=nb, rows=rows, n_new=n_new, gd=gd)
    return pl.pallas_call(
        body,
        grid=(db // nb,),
        in_specs=[pl.BlockSpec((nb * rows, pd), lambda i: (i, 0)), pl.BlockSpec((nb * n_new, d), lambda i: (i, 1)),
                  _full(w["pool_lin"].shape), _full(w["pool_scale"].shape), _full(w["wpbr"].shape)],
        out_specs=pl.BlockSpec((nb * n_new, d), lambda i: (i, 0)),
        out_shape=jax.ShapeDtypeStruct((db * n_new, d), BF16),
        compiler_params=_params(dimension_semantics=("parallel",)),
        name="pool_sample",
    )(ext2d, gates, w["pool_lin"], w["pool_scale"], w["wpbr"])


def _flash_body(q_ref, k_ref, v_ref, o_ref, m_sc, l_sc, acc_sc, *, tq, tk):
    i = pl.program_id(2)
    j = pl.program_id(3)

    @pl.when(j == 0)
    def _():
        m_sc[...] = jnp.full(m_sc.shape, NEG, F32)
        l_sc[...] = jnp.zeros(l_sc.shape, F32)
        acc_sc[...] = jnp.zeros(acc_sc.shape, F32)

    @pl.when(j * tk <= i * tq + tq - 1)
    def _():
        s = _nt(q_ref[...], k_ref[...])
        qpos = i * tq + lax.broadcasted_iota(jnp.int32, s.shape, 0)
        kpos = j * tk + lax.broadcasted_iota(jnp.int32, s.shape, 1)
        s = jnp.where(kpos <= qpos, s, NEG)
        m_old = m_sc[...]
        m_new = jnp.maximum(m_old, jnp.max(s, axis=-1, keepdims=True))
        corr = jnp.exp(m_old - m_new)
        p = jnp.exp(s - m_new)
        l_sc[...] = l_sc[...] * corr + jnp.sum(p, axis=-1, keepdims=True)
        acc_sc[...] = acc_sc[...] * corr + _mm(p.astype(BF16), v_ref[...])
        m_sc[...] = m_new

    @pl.when(j == pl.num_programs(3) - 1)
    def _():
        o_ref[...] = (acc_sc[...] / l_sc[...]).astype(BF16)


def _flash(q, k, v, batch, seq, n_heads, tq, tk):
    n = q.shape[0]
    nq = seq // tq
    nk = seq // tk
    last = lambda i: (i * tq + tq - 1) // tk
    qmap = lambda b, h, i, j: (b * nq + i, h)
    kmap = lambda b, h, i, j: (b * nk + jnp.minimum(j, last(i)), h)
    body = functools.partial(_flash_body, tq=tq, tk=tk)
    return pl.pallas_call(
        body,
        grid=(batch, n_heads, nq, nk),
        in_specs=[pl.BlockSpec((tq, HEAD_PAD), qmap), pl.BlockSpec((tk, HEAD_PAD), kmap),
                  pl.BlockSpec((tk, HEAD_PAD), kmap)],
        out_specs=pl.BlockSpec((tq, HEAD_PAD), qmap),
        out_shape=jax.ShapeDtypeStruct((n, n_heads * HEAD_PAD), BF16),
        scratch_shapes=[pltpu.VMEM((tq, 1), F32), pltpu.VMEM((tq, 1), F32), pltpu.VMEM((tq, HEAD_PAD), F32)],
        compiler_params=_params(dimension_semantics=("parallel", "parallel", "parallel", "arbitrary")),
        name="flash",
    )(q, k, v)


def _sattn_body(pt_ref, q_ref, cn_ref, krn_ref, wukt_ref, wukh_ref, knw_ref, wuv_ref, *rest,
                n_pg, n_heads, n_q, d_nope, d_rope, d_head):
    c_refs = rest[:n_pg]
    kr_refs = rest[n_pg:2 * n_pg]
    o_ref = rest[2 * n_pg]
    qc_sc, qp_sc, m_sc, l_sc, acc_sc = rest[2 * n_pg + 1:]
    s_idx = pl.program_id(1)
    kv_lora = acc_sc.shape[1]

    @pl.when(s_idx == 0)
    def _():
        knw = knw_ref[...]
        for h in range(n_heads):
            qp = q_ref[:, h * HEAD_PAD:(h + 1) * HEAD_PAD] * knw
            qp_sc[h * n_q:(h + 1) * n_q, :] = qp
            qc_sc[h * n_q:(h + 1) * n_q, :] = _nt(qp.astype(BF16), wukh_ref[h])
        m_sc[...] = jnp.full(m_sc.shape, NEG, F32)
        l_sc[...] = jnp.zeros(l_sc.shape, F32)
        acc_sc[...] = jnp.zeros(acc_sc.shape, F32)

    def update(cb, s_rope, krs, mask):
        knt = _nt(wukt_ref[...], cb)
        sq = knt * knt
        ssq = jnp.sum(sq.reshape(n_heads, d_nope, sq.shape[-1]), axis=1) + krs
        r = lax.rsqrt(ssq * (1.0 / d_head) + EPS)
        s_raw = _nt(qc_sc[...].astype(BF16), cb) + s_rope
        s = jnp.concatenate([s_raw[h * n_q:(h + 1) * n_q, :] * r[h:h + 1, :] for h in range(n_heads)], axis=0)
        if mask is not None:
            s = jnp.where(mask, s, NEG)
        m_old = m_sc[...]
        m_new = jnp.maximum(m_old, jnp.max(s, axis=-1, keepdims=True))
        corr = jnp.exp(m_old - m_new)
        p = jnp.exp(s - m_new)
        l_sc[...] = l_sc[...] * corr + jnp.sum(p, axis=-1, keepdims=True)
        acc_sc[...] = acc_sc[...] * corr + _mm(p.astype(BF16), cb)
        m_sc[...] = m_new

    def split_sum_rows(x):
        hi = x.astype(BF16)
        lo = (x - hi.astype(F32)).astype(BF16)
        ones = jnp.ones((n_heads, x.shape[1]), BF16)
        return _nt(ones, hi) + _nt(ones, lo)

    c = jnp.concatenate([r[0, 0] for r in c_refs], axis=0)
    kr = jnp.concatenate([r[0, 0] for r in kr_refs], axis=0)
    qr = qp_sc[:, d_nope:d_nope + d_rope].astype(BF16)
    update(c.astype(BF16), _nt(qr, kr.astype(BF16)), split_sum_rows(kr * kr), None)

    @pl.when(s_idx == pl.num_programs(1) - 1)
    def _():
        cn = cn_ref[0]
        krn = krn_ref[0]
        rows = n_heads * n_q
        qpos = lax.broadcasted_iota(jnp.int32, (rows, cn.shape[0]), 0) % n_q
        tpos = lax.broadcasted_iota(jnp.int32, (rows, cn.shape[0]), 1)
        update(cn.astype(BF16), _nt(qp_sc[...].astype(BF16), krn.astype(BF16)), split_sum_rows(krn * krn),
               tpos <= qpos)
        lat = (acc_sc[...] / l_sc[...]).astype(BF16)
        for h in range(n_heads):
            sl = slice(h * HEAD_PAD, (h + 1) * HEAD_PAD)
            o_ref[:, sl] = _mm(lat[h * n_q:(h + 1) * n_q, :], wuv_ref[:, sl])


def _sattn(page_table, q, c_new_pad, kr_new_pad, ckv, krope, layer, w, dims, n_pg):
    db, n_pages = page_table.shape
    n_q = q.shape[0] // db
    page, kv_lora = ckv.shape[2], ckv.shape[3]
    d_rope = krope.shape[3]
    n_heads = dims["n_heads"]
    hp = n_heads * HEAD_PAD
    steps = n_pages // n_pg
    rows = n_heads * n_q
    body = functools.partial(_sattn_body, n_pg=n_pg, n_heads=n_heads, n_q=n_q, d_nope=dims["d_nope"],
                             d_rope=d_rope, d_head=dims["d_head"])
    pt = page_table.reshape(-1)

    def page_map(p):
        return lambda b, s, pt_ref: (layer, pt_ref[b * n_pages + s * n_pg + p], 0, 0)

    fixed = lambda shape: pl.BlockSpec(shape, lambda b, s, pt_ref: (0,) * len(shape))
    grid_spec = pltpu.PrefetchScalarGridSpec(
        num_scalar_prefetch=1,
        grid=(db, steps),
        in_specs=[pl.BlockSpec((n_q, hp), lambda b, s, pt_ref: (b, 0)),
                  pl.BlockSpec((1, page, kv_lora), lambda b, s, pt_ref: (b, 0, 0)),
                  pl.BlockSpec((1, page, HEAD_PAD), lambda b, s, pt_ref: (b, 0, 0)),
                  fixed(w["wukt"].shape), fixed(w["wukh"].shape), fixed(w["knw_raw"].shape), fixed(w["wuv"].shape)]
                 + [pl.BlockSpec((1, 1, page, kv_lora), page_map(p)) for p in range(n_pg)]
                 + [pl.BlockSpec((1, 1, page, d_rope), page_map(p)) for p in range(n_pg)],
        out_specs=pl.BlockSpec((n_q, hp), lambda b, s, pt_ref: (b, 0)),
        scratch_shapes=[pltpu.VMEM((rows, kv_lora), F32), pltpu.VMEM((rows, HEAD_PAD), F32),
                        pltpu.VMEM((rows, 1), F32), pltpu.VMEM((rows, 1), F32), pltpu.VMEM((rows, kv_lora), F32)],
    )
    return pl.pallas_call(
        body,
        grid_spec=grid_spec,
        out_shape=jax.ShapeDtypeStruct((db * n_q, hp), F32),
        compiler_params=_params(dimension_semantics=("parallel", "arbitrary")),
        name="sattn",
    )(pt, q, c_new_pad, kr_new_pad, w["wukt"], w["wukh"], w["knw_raw"], w["wuv"],
      *([ckv] * n_pg), *([krope] * n_pg))


def _post_body(x_ref, a_ref, g_ref, pc_ref, wab_ref, wout_ref, h_ref):
    br = _mm(a_ref[...].astype(BF16), wab_ref[...])
    mixed = g_ref[...].astype(F32) * br + pc_ref[...].astype(F32)
    h_ref[...] = x_ref[...] + _mm(mixed.astype(BF16), wout_ref[...])


def _post(x2d, attn, gates, pc, w, tm):
    n, d = x2d.shape
    row = lambda i: (i, 0)
    return pl.pallas_call(
        _post_body,
        grid=(n // tm,),
        in_specs=[pl.BlockSpec((tm, d), row), pl.BlockSpec((tm, attn.shape[1]), row), pl.BlockSpec((tm, d), row),
                  pl.BlockSpec((tm, d), row), _full(w["wab"].shape), _full(w["wout"].shape)],
        out_specs=pl.BlockSpec((tm, d), row),
        out_shape=jax.ShapeDtypeStruct((n, d), F32),
        compiler_params=_params(dimension_semantics=("parallel",)),
        name="post",
    )(x2d, attn, gates, pc, w["wab"], w["wout"])


def _gelu(x):
    return x * (0.5 * (1.0 + jnp.tanh(math.sqrt(2.0 / math.pi) * (x + 0.044715 * (x * x * x)))))


def _top_rows(s, k):
    rid = lax.broadcasted_iota(jnp.int32, (k, s.shape[1]), 0)

    def step(i, carry):
        work, tops = carry
        m = jnp.max(work, axis=0, keepdims=True)
        tops = jnp.where(rid == i, m, tops)
        work = jnp.where(work == m, -jnp.inf, work)
        return work, tops

    _, tops = lax.fori_loop(0, k, step, (s, jnp.full((k, s.shape[1]), -jnp.inf, F32)))
    return tops


def _peer_body(h_ref, n2_ref, wqt_ref, keys_ref, u_ref, vt_ref, y_ref,
               hn_sc, s1_sc, e1_sc, th_sc, e0_sc, at_sc, g_sc, acc_sc, *, n_heads, n_keys, irows):
    e = pl.program_id(1)
    k = PEER_TOPK
    t = hn_sc.shape[0]

    @pl.when(e == 0)
    def _():
        hn_sc[...] = _rms(h_ref[...], n2_ref[...]).astype(BF16)
        acc_sc[...] = jnp.zeros(acc_sc.shape, F32)

        def head(h, _):
            def scores(p):
                r0 = pl.multiple_of((h * 2 + p) * n_keys, n_keys)
                qt = _nt(wqt_ref[pl.ds(r0, n_keys), :], hn_sc[...])
                return _mm(keys_ref[h * 2 + p], qt.astype(BF16))

            s0 = scores(0)
            s1 = scores(1)
            top0 = _top_rows(s0, k)
            top1 = _top_rows(s1, k)
            cand = jnp.concatenate(
                [top0[0:1] + top1]
                + [top0[a:a + 1] + top1[0:8] for a in range(1, 8)]
                + [top0[8:16] + top1[0:1]], axis=0)
            ctop = _top_rows(cand, k + 1)
            c_k = ctop[k - 1:k]
            tau = c_k + 0.5 * (ctop[k:k + 1] - c_k)
            m = top0[0:1] + top1[0:1]
            z = jnp.sum(jnp.where(cand >= c_k, jnp.exp(cand - m), 0.0), axis=0, keepdims=True)
            in0 = s0 >= top0[k - 1:k]
            e0_sc[h] = jnp.where(in0, jnp.exp(s0 - top0[0:1]) / z, 0.0)
            th_sc[h] = jnp.where(in0, jnp.maximum(tau - s0, top1[k - 1:k]), jnp.inf)
            e1_sc[h] = jnp.exp(s1 - top1[0:1])
            s1_sc[h] = s1
            return 0

        lax.fori_loop(0, n_heads, head, 0)

    at_sc[...] = _nt(u_ref[...], hn_sc[...])

    base = pl.multiple_of(e * irows, irows)
    for lt in range(t // LANES):
        ls = slice(lt * LANES, (lt + 1) * LANES)
        th_t = [th_sc[h, pl.ds(base, irows), ls] for h in range(n_heads)]
        e0_t = [e0_sc[h, pl.ds(base, irows), ls] for h in range(n_heads)]
        for ii in range(irows):
            rs = slice(ii * n_keys, (ii + 1) * n_keys)
            g = _gelu(at_sc[rs, ls])
            wgt = jnp.zeros((n_keys, LANES), F32)
            for h in range(n_heads):
                sel = s1_sc[h, :, ls] >= th_t[h][ii:ii + 1, :]
                wgt = wgt + jnp.where(sel, e1_sc[h, :, ls] * e0_t[h][ii:ii + 1, :], 0.0)
            g_sc[rs, ls] = (wgt * g).astype(BF16)
    acc_sc[...] += _mm(vt_ref[...], g_sc[...])

    @pl.when(e == pl.num_programs(1) - 1)
    def _():
        y_ref[...] = h_ref[...] + acc_sc[...].T


def _peer(h2d, w, tt, et):
    n, d = h2d.shape
    n_heads, n_keys = w["n_peer_heads"], w["n_keys"]
    n_exp = w["pu"].shape[0]
    irows = et // n_keys
    body = functools.partial(_peer_body, n_heads=n_heads, n_keys=n_keys, irows=irows)
    small = pltpu.VMEM((n_heads, n_keys, tt), F32)
    return pl.pallas_call(
        body,
        grid=(n // tt, n_exp // et),
        in_specs=[pl.BlockSpec((tt, d), lambda i, e: (i, 0)), _full(w["n2"].shape), _full(w["wqt"].shape),
                  _full(w["keys"].shape), pl.BlockSpec((et, d), lambda i, e: (e, 0)),
                  pl.BlockSpec((d, et), lambda i, e: (0, e))],
        out_specs=pl.BlockSpec((tt, d), lambda i, e: (i, 0)),
        out_shape=jax.ShapeDtypeStruct((n, d), F32),
        scratch_shapes=[pltpu.VMEM((tt, d), BF16), small, small, small, small,
                        pltpu.VMEM((et, tt), F32), pltpu.VMEM((et, tt), BF16), pltpu.VMEM((d, tt), F32)],
        compiler_params=_params(dimension_semantics=("parallel", "arbitrary")),
        name="peer",
    )(h2d, w["n2"], w["wqt"], w["keys"], w["pu"], w["pvt"])


def _rot_cols(wr):
    half = wr.shape[-1] // 2
    return jnp.concatenate([-wr[..., half:], wr[..., :half]], axis=-1)


def _rope_tables(pos, d_nope, d_rope):
    half = d_rope // 2
    inv_freq = ROPE_THETA ** (-jnp.arange(half, dtype=F32) / half)
    ang = pos.astype(F32)[:, None] * inv_freq[None, :]
    cos = jnp.cos(ang)
    sin = jnp.sin(ang)
    n = pos.shape[0]
    pad = jnp.zeros((n, HEAD_PAD - d_nope - d_rope), F32)
    ta = jnp.concatenate([jnp.ones((n, d_nope), F32), cos, cos, pad], axis=-1)
    tb = jnp.concatenate([jnp.zeros((n, d_nope), F32), sin, sin, pad], axis=-1)
    return ta, tb


def _layer_weights(l, norm1_w, w_in, q_lat_norm_w, w_q_up, kv_lat_norm_w, w_uk, w_uv, q_norm_w, k_norm_w,
                   w_attn_br, pool_lin_w, pool_scale, w_pool_br, w_out, norm2_w, peer_w_q, peer_keys,
                   peer_u, peer_v, d_rope):
    d = w_in.shape[1]
    q_lora = w_q_up.shape[1]
    n_heads, d_head = w_q_up.shape[2], w_q_up.shape[3]
    kv_lora = w_uk.shape[1]
    d_nope = w_uk.shape[3]
    d_v = w_uv.shape[3]
    off_kv = q_lora
    off_kr = off_kv + kv_lora
    off_pool = off_kr + d_rope
    scale = d_head ** -0.5
    wi = w_in[l]
    w_kr = wi[:, off_kr:off_pool]
    krblk = jnp.concatenate([jnp.zeros((d, d_nope), F32), w_kr, _rot_cols(w_kr)], axis=-1)
    win = jnp.concatenate([wi[:, :off_kr], krblk, wi[:, off_pool:]], axis=-1).astype(BF16)
    wq = w_q_up[l]
    wq = jnp.concatenate([wq[..., :d_nope], wq[..., d_nope:], _rot_cols(wq[..., d_nope:])], axis=-1)
    wq = wq.reshape(q_lora, n_heads * HEAD_PAD).astype(BF16)
    padk = lambda a, width: jnp.concatenate([a, jnp.zeros(a.shape[:-1] + (HEAD_PAD - width,), F32)], axis=-1)
    wuk_pad = padk(w_uk[l], d_nope)
    wuv_pad = padk(w_uv[l], d_v)
    wab = jnp.concatenate([w_attn_br[l].reshape(n_heads, d_v, d),
                           jnp.zeros((n_heads, HEAD_PAD - d_v, d), F32)], axis=1).reshape(n_heads * HEAD_PAD, d)
    n_ph, _, n_keys, half = peer_keys.shape[1:]
    return dict(
        n1=norm1_w[l][None, :], win=win, qlw=q_lat_norm_w[l][None, :], wq=wq, kvw=kv_lat_norm_w[l][None, :],
        wuk=wuk_pad.reshape(kv_lora, n_heads * HEAD_PAD).astype(BF16),
        wuv=wuv_pad.reshape(kv_lora, n_heads * HEAD_PAD).astype(BF16),
        qnw=padk(q_norm_w[l], d_head)[None, :] * scale, knw=padk(k_norm_w[l], d_head)[None, :],
        knw_raw=padk(k_norm_w[l], d_head)[None, :],
        wukt=jnp.transpose(w_uk[l], (1, 2, 0)).reshape(n_heads * d_nope, kv_lora).astype(BF16),
        wukh=jnp.transpose(wuk_pad, (1, 0, 2)).astype(BF16),
        wab=wab.astype(BF16), wout=w_out[l].astype(BF16),
        pool_lin=pool_lin_w[l].astype(BF16), pool_scale=pool_scale[l][None, :], wpbr=w_pool_br[l].astype(BF16),
        n2=norm2_w[l][None, :], wqt=peer_w_q[l].T.astype(BF16),
        keys=peer_keys[l].reshape(n_ph * 2, n_keys, half).astype(BF16),
        pu=peer_u[l].astype(BF16), pvt=peer_v[l].T.astype(BF16),
        n_peer_heads=n_ph, n_keys=n_keys,
    ), dict(q_lora=q_lora, kv_lora=kv_lora, pool_dim=off_pool_dim(wi, off_pool, d), n_heads=n_heads,
            d_head=d_head, d_nope=d_nope)


def off_pool_dim(wi, off_pool, d):
    return wi.shape[1] - off_pool - 2 * d


def _tile(n, pref):
    t = pref
    while n % t:
        t //= 2
    return t


def kernel(x_prompt, x_sample, cache_ckv, cache_krope, state_pool, page_table, norm1_w, w_in, q_lat_norm_w, w_q_up, kv_lat_norm_w, w_uk, w_uv, q_norm_w, k_norm_w, w_attn_br, pool_lin_w, pool_scale, w_pool_br, w_out, norm2_w, peer_w_q, peer_keys, peer_u, peer_v):
    batch, seq, d = x_prompt.shape
    db, n_q, _ = x_sample.shape
    depth = w_in.shape[0]
    page = cache_ckv.shape[2]
    d_rope = cache_krope.shape[3]
    n_pages = page_table.shape[1]
    past_len = n_pages * page
    n_state = state_pool.shape[2]
    n_p, n_s = batch * seq, db * n_q
    xp = x_prompt.reshape(n_p, d)
    xs = x_sample.reshape(n_s, d)
    outs = [[] for _ in range(6)]
    for l in range(depth):
        w, dims = _layer_weights(l, norm1_w, w_in, q_lat_norm_w, w_q_up, kv_lat_norm_w, w_uk, w_uv, q_norm_w,
                                 k_norm_w, w_attn_br, pool_lin_w, pool_scale, w_pool_br, w_out, norm2_w,
                                 peer_w_q, peer_keys, peer_u, peer_v, d_rope)
        d_nope = dims["d_nope"]
        kv_lora = dims["kv_lora"]
        pool_dim = dims["pool_dim"]
        rope_sl = slice(d_nope, d_nope + d_rope)
        tm_p = _tile(seq, 256)
        tm_s = _tile(n_s, 256)
        tt_p = _tile(n_p, 512)
        tt_s = _tile(n_s, 512)
        et = _tile(w["pu"].shape[0], 1024)

        ta, tb = _rope_tables(jnp.arange(seq), d_nope, d_rope)
        q, k, v, c, krp, u, gates = _proj(xp, ta, tb, w, dims, tm_p)
        attn = _flash(q, k, v, batch, seq, dims["n_heads"], _tile(seq, 512), _tile(seq, 512))
        pc = _pool_prompt(u, gates, w, batch, seq, tm_p)
        h = _post(xp, attn, gates, pc, w, tt_p)
        xp = _peer(h, w, tt_p, et)
        outs[0].append(c.reshape(batch, seq, kv_lora))
        outs[1].append(krp[:, rope_sl].reshape(batch, seq, d_rope))
        outs[2].append(u.reshape(batch, seq, pool_dim)[:, seq - n_state:])

        pos_s = past_len + jnp.arange(n_q)
        ta, tb = _rope_tables(jnp.tile(pos_s, tm_s // n_q), d_nope, d_rope)
        q, _, _, c, krp, u, gates = _proj(xs, ta, tb, w, dims, tm_s)
        zrow = lambda a, rows: jnp.concatenate(
            [a, jnp.zeros((a.shape[0], rows - a.shape[1], a.shape[2]), a.dtype)], axis=1)
        c_new = zrow(c.reshape(db, n_q, kv_lora), page)
        kr_new = zrow(krp.reshape(db, n_q, HEAD_PAD), page)
        n_pg = _tile(n_pages, 8)
        attn = _sattn(page_table, q.astype(F32), c_new, kr_new, cache_ckv, cache_krope, l, w, dims, n_pg)
        u3 = u.reshape(db, n_q, pool_dim)
        u_full = jnp.concatenate([state_pool[l], u3], axis=1)
        rows = -(-(n_state + n_q + 1) // 8) * 8
        ext = jnp.concatenate([jnp.zeros((db, rows - n_state - n_q, pool_dim), F32), u_full], axis=1)
        pc = _pool_sample(ext.reshape(db * rows, pool_dim), gates, w, db, rows, n_q)
        h = _post(xs, attn, gates, pc, w, tt_s)
        xs = _peer(h, w, tt_s, et)
        outs[3].append(c.reshape(db, n_q, kv_lora))
        outs[4].append(krp[:, rope_sl].reshape(db, n_q, d_rope))
        outs[5].append(u_full[:, -n_state:])
    return (xp.reshape(batch, seq, d), xs.reshape(db, n_q, d)) + tuple(jnp.stack(o) for o in outs)
```

```python
import functools
import math

import jax
import jax.numpy as jnp
from jax import lax
from jax.experimental import pallas as pl
from jax.experimental.pallas import tpu as pltpu

F32 = jnp.float32
BF16 = jnp.bfloat16

EPS = 1e-6
ROPE_THETA = 10000.0
POOL_WINDOWS = (2, 4, 8, 16)
PEER_TOPK = 16
NEG = -1e30
LANES = 128
HEAD_PAD = 128
VMEM_LIMIT = 56 * 1024 * 1024


def _nt(a, b):
    return lax.dot_general(a, b, (((1,), (1,)), ((), ())), preferred_element_type=F32)


def _mm(a, b):
    return jnp.dot(a, b, preferred_element_type=F32)


def _rms(x, w):
    return x * lax.rsqrt(jnp.mean(x * x, axis=-1, keepdims=True) + EPS) * w


def _params(**kw):
    return pltpu.CompilerParams(vmem_limit_bytes=VMEM_LIMIT, **kw)


def _full(shape):
    n = len(shape)
    return pl.BlockSpec(shape, lambda *_: (0,) * n)


def _head_norm(blk, ta, tb, w, d_head):
    rot = pltpu.roll(blk, HEAD_PAD - 32, axis=1)
    hq = blk * ta + rot * tb
    ssq = jnp.sum(hq * hq, axis=-1, keepdims=True)
    return hq * lax.rsqrt(ssq * (1.0 / d_head) + EPS) * w


def _proj_body(x_ref, ta_ref, tb_ref, n1_ref, win_ref, qlw_ref, wq_ref, kvw_ref, wuk_ref, wuv_ref,
               qnw_ref, knw_ref, q_ref, k_ref, v_ref, c_ref, kr_ref, u_ref, g_ref,
               *, q_lora, kv_lora, pool_dim, n_heads, d_head, d_v):
    x = x_ref[...]
    xn = _rms(x, n1_ref[...]).astype(BF16)
    proj = _mm(xn, win_ref[...])
    o1 = q_lora
    o2 = o1 + kv_lora
    o3 = o2 + HEAD_PAD
    o4 = o3 + pool_dim
    ta = ta_ref[...]
    tb = tb_ref[...]

    ql = _rms(proj[:, :o1], qlw_ref[...]).astype(BF16)
    qu = _mm(ql, wq_ref[...])
    c = _rms(proj[:, o1:o2], kvw_ref[...])
    c_ref[...] = c
    cb = c.astype(BF16)
    krb = proj[:, o2:o3]
    krp = krb * ta + pltpu.roll(krb, HEAD_PAD - 32, axis=1) * tb
    kr_ref[...] = krp
    ku = _mm(cb, wuk_ref[...])
    vt = _nt(wuv_ref[...], cb)
    vrow = lax.broadcasted_iota(jnp.int32, vt.shape, 0) % HEAD_PAD
    v_ref[0] = jnp.where(vrow == d_v, 1.0, vt).astype(BF16)
    qnw = qnw_ref[...]
    knw = knw_ref[...]
    for h in range(n_heads):
        sl = slice(h * HEAD_PAD, (h + 1) * HEAD_PAD)
        q_ref[:, sl] = _head_norm(qu[:, sl], ta, tb, qnw, d_head).astype(BF16)
        kh = ku[:, sl] + krp
        ssq = jnp.sum(kh * kh, axis=-1, keepdims=True)
        k_ref[:, sl] = (kh * lax.rsqrt(ssq * (1.0 / d_head) + EPS) * knw).astype(BF16)
    u_ref[...] = proj[:, o3:o4]
    g_ref[...] = jax.nn.sigmoid(proj[:, o4:]).astype(BF16)


def _proj(x2d, ta, tb, w, dims, tm):
    n, d = x2d.shape
    nt = n // tm
    tab_blocks = ta.shape[0] // tm
    row = lambda i: (i, 0)
    tab = lambda i: (i % tab_blocks, 0)
    hp = dims["n_heads"] * HEAD_PAD
    body = functools.partial(_proj_body, q_lora=dims["q_lora"], kv_lora=dims["kv_lora"],
                             pool_dim=dims["pool_dim"], n_heads=dims["n_heads"], d_head=dims["d_head"],
                             d_v=dims["d_v"])
    consts = [w["n1"], w["win"], w["qlw"], w["wq"], w["kvw"], w["wuk"], w["wuvt"], w["qnw"], w["knw"]]
    return pl.pallas_call(
        body,
        grid=(nt,),
        in_specs=[pl.BlockSpec((tm, d), row), pl.BlockSpec((tm, HEAD_PAD), tab), pl.BlockSpec((tm, HEAD_PAD), tab)]
                 + [_full(a.shape) for a in consts],
        out_specs=[pl.BlockSpec((tm, hp), row), pl.BlockSpec((tm, hp), row),
                   pl.BlockSpec((1, hp, tm), lambda i: (i, 0, 0)),
                   pl.BlockSpec((tm, dims["kv_lora"]), row), pl.BlockSpec((tm, HEAD_PAD), row),
                   pl.BlockSpec((tm, dims["pool_dim"]), row), pl.BlockSpec((tm, 2 * d), row)],
        out_shape=[jax.ShapeDtypeStruct((n, hp), BF16), jax.ShapeDtypeStruct((n, hp), BF16),
                   jax.ShapeDtypeStruct((nt, hp, tm), BF16), jax.ShapeDtypeStruct((n, dims["kv_lora"]), F32),
                   jax.ShapeDtypeStruct((n, HEAD_PAD), F32), jax.ShapeDtypeStruct((n, dims["pool_dim"]), F32),
                   jax.ShapeDtypeStruct((n, 2 * d), BF16)],
        compiler_params=_params(dimension_semantics=("parallel",)),
        name="proj",
    )(x2d, ta, tb, *consts)


def _window_sums(ext):
    s2 = ext + pltpu.roll(ext, 1, axis=0)
    s4 = s2 + pltpu.roll(s2, 2, axis=0)
    s8 = s4 + pltpu.roll(s4, 4, axis=0)
    s16 = s8 + pltpu.roll(s8, 8, axis=0)
    return (s2, s4, s8, s16)


def _pool_tail(z_groups, g2, lin_ref, scale_ref, wbr_ref, out_ref):
    zl = [_mm(z.astype(BF16), lin_ref[g]) for g, z in enumerate(z_groups)]
    pool = jnp.concatenate(zl, axis=-1) * scale_ref[...]
    br = _mm(pool.astype(BF16), wbr_ref[...])
    out_ref[...] = (g2.astype(F32) * br).astype(BF16)


def _pool_prompt_body(u_ref, halo_ref, g_ref, lin_ref, scale_ref, wbr_ref, out_ref, *, tm, gd):
    j = pl.program_id(1)
    u = u_ref[...]
    prev = jnp.where(j == 0, 0.0, halo_ref[...])
    sums = _window_sums(jnp.concatenate([prev, u], axis=0))
    pos = j * tm + lax.broadcasted_iota(jnp.int32, (tm, gd), 0)
    zs = []
    for g, wdw in enumerate(POOL_WINDOWS):
        sl = slice(g * gd, (g + 1) * gd)
        cnt = jnp.minimum(pos + 1, wdw).astype(F32)
        zs.append(sums[g][16:, sl] / cnt - u[:, sl])
    _pool_tail(zs, g_ref[...], lin_ref, scale_ref, wbr_ref, out_ref)


def _pool_prompt(u2d, gates, w, batch, seq, tm):
    n, pd = u2d.shape
    d = gates.shape[1] // 2
    ns = seq // tm
    gd = pd // len(POOL_WINDOWS)
    hb = tm // 16
    body = functools.partial(_pool_prompt_body, tm=tm, gd=gd)
    return pl.pallas_call(
        body,
        grid=(batch, ns),
        in_specs=[pl.BlockSpec((tm, pd), lambda b, j: (b * ns + j, 0)),
                  pl.BlockSpec((16, pd), lambda b, j: (jnp.maximum((b * ns + j) * hb - 1, 0), 0)),
                  pl.BlockSpec((tm, d), lambda b, j: (b * ns + j, 1)),
                  _full(w["pool_lin"].shape), _full(w["pool_scale"].shape), _full(w["wpbr"].shape)],
        out_specs=pl.BlockSpec((tm, d), lambda b, j: (b * ns + j, 0)),
        out_shape=jax.ShapeDtypeStruct((n, d), BF16),
        compiler_params=_params(dimension_semantics=("parallel", "parallel")),
        name="pool_prompt",
    )(u2d, u2d, gates, w["pool_lin"], w["pool_scale"], w["wpbr"])


def _pool_sample_body(ext_ref, g_ref, lin_ref, scale_ref, wbr_ref, out_ref, *, nb, rows, n_new, gd):
    ext = ext_ref[...]
    sums = _window_sums(ext)
    pick = lambda a: a.reshape(nb, rows, a.shape[-1])[:, rows - n_new:, :].reshape(nb * n_new, a.shape[-1])
    zs = []
    for g, wdw in enumerate(POOL_WINDOWS):
        sl = slice(g * gd, (g + 1) * gd)
        zs.append(pick(sums[g][:, sl]) / float(wdw) - pick(ext[:, sl]))
    _pool_tail(zs, g_ref[...], lin_ref, scale_ref, wbr_ref, out_ref)


def _pool_sample(ext2d, gates, w, db, rows, n_new):
    pd = ext2d.shape[1]
    d = gates.shape[1] // 2
    gd = pd // len(POOL_WINDOWS)
    nb = _tile(db, 16)
    body = functools.partial(_pool_sample_body, nb=nb, rows=rows, n_new=n_new, gd=gd)
    return pl.pallas_call(
        body,
        grid=(db // nb,),
        in_specs=[pl.BlockSpec((nb * rows, pd), lambda i: (i, 0)), pl.BlockSpec((nb * n_new, d), lambda i: (i, 1)),
                  _full(w["pool_lin"].shape), _full(w["pool_scale"].shape), _full(w["wpbr"].shape)],
        out_specs=pl.BlockSpec((nb * n_new, d), lambda i: (i, 0)),
        out_shape=jax.ShapeDtypeStruct((db * n_new, d), BF16),
        compiler_params=_params(dimension_semantics=("parallel",)),
        name="pool_sample",
    )(ext2d, gates, w["pool_lin"], w["pool_scale"], w["wpbr"])


def _flash_body(q_ref, k_ref, vt_ref, o_ref, m_sc, acc_sc, *, tq, hps, l_row):
    i = pl.program_id(2)
    tv = vt_ref.shape[2]
    m_sc[...] = jnp.full(m_sc.shape, NEG, F32)
    acc_sc[...] = jnp.zeros(acc_sc.shape, F32)

    def chunk(j, masked):
        r0 = pl.multiple_of(j * tq, tq)
        for hh in range(hps):
            sl = slice(hh * HEAD_PAD, (hh + 1) * HEAD_PAD)
            st = _nt(k_ref[pl.ds(r0, tq), sl], q_ref[:, sl])
            if masked:
                kpos = lax.broadcasted_iota(jnp.int32, st.shape, 0)
                qpos = lax.broadcasted_iota(jnp.int32, st.shape, 1)
                st = jnp.where(kpos <= qpos, st, NEG)
            m_old = m_sc[hh]
            m_new = jnp.maximum(m_old, jnp.max(st, axis=0, keepdims=True))
            p = jnp.exp(st - m_new).astype(BF16)
            pv = _mm(vt_ref[j * (tq // tv), sl, :], p[0:tv])
            for c in range(1, tq // tv):
                pv = pv + _mm(vt_ref[j * (tq // tv) + c, sl, :], p[c * tv:(c + 1) * tv])
            acc_sc[hh] = acc_sc[hh] * jnp.exp(m_old - m_new) + pv
            m_sc[hh] = m_new

    def full_chunk(j, carry):
        chunk(j, False)
        return carry

    lax.fori_loop(0, i, full_chunk, 0)
    chunk(i, True)
    for hh in range(hps):
        acc = acc_sc[hh]
        o_ref[:, hh * HEAD_PAD:(hh + 1) * HEAD_PAD] = (acc / acc[l_row:l_row + 1, :]).T.astype(BF16)


def _flash(q, k, vt, batch, seq, n_heads, tq, hps, l_row):
    n = q.shape[0]
    nq = seq // tq
    w = hps * HEAD_PAD
    tv = vt.shape[2]
    body = functools.partial(_flash_body, tq=tq, hps=hps, l_row=l_row)
    return pl.pallas_call(
        body,
        grid=(batch, n_heads // hps, nq),
        in_specs=[pl.BlockSpec((tq, w), lambda b, h, i: (b * nq + i, h)),
                  pl.BlockSpec((seq, w), lambda b, h, i: (b, h)),
                  pl.BlockSpec((seq // tv, w, tv), lambda b, h, i: (b, h, 0))],
        out_specs=pl.BlockSpec((tq, w), lambda b, h, i: (b * nq + i, h)),
        out_shape=jax.ShapeDtypeStruct((n, n_heads * HEAD_PAD), BF16),
        scratch_shapes=[pltpu.VMEM((hps, 1, tq), F32), pltpu.VMEM((hps, HEAD_PAD, tq), F32)],
        compiler_params=_params(dimension_semantics=("parallel", "parallel", "arbitrary")),
        name="flash",
    )(q, k, vt)


def _sattn_body(pt_ref, q_ref, cn_ref, krn_ref, wukt_ref, wukh_ref, knw_ref, wuv_ref, *rest,
                n_pg, pg_chunk, n_heads, n_q, d_nope, d_rope, d_head):
    c_refs = rest[:n_pg]
    kr_refs = rest[n_pg:2 * n_pg]
    o_ref = rest[2 * n_pg]
    wall_sc, qp_sc, m_sc, l_sc, acc_sc = rest[2 * n_pg + 1:]
    s_idx = pl.program_id(1)
    n_kn = n_heads * d_nope
    rows = n_heads * n_q

    @pl.when(s_idx == 0)
    def _():
        knw = knw_ref[...]
        qcs = []
        for h in range(n_heads):
            qp = q_ref[:, h * HEAD_PAD:(h + 1) * HEAD_PAD] * knw
            qp_sc[h * n_q:(h + 1) * n_q, :] = qp
            qcs.append(_nt(qp.astype(BF16), wukh_ref[h]))
        wall_sc[0:n_kn, :] = wukt_ref[...]
        wall_sc[n_kn:n_kn + rows, :] = jnp.concatenate(qcs, axis=0).astype(BF16)
        m_sc[...] = jnp.full(m_sc.shape, NEG, F32)
        l_sc[...] = jnp.zeros(l_sc.shape, F32)
        acc_sc[...] = jnp.zeros(acc_sc.shape, F32)

    def scores(cb, s_rope, krs):
        big = _nt(wall_sc[...], cb)
        knt = big[0:n_kn]
        sq = knt * knt
        ssq = jnp.sum(sq.reshape(n_heads, d_nope, sq.shape[-1]), axis=1) + krs
        r = lax.rsqrt(ssq * (1.0 / d_head) + EPS)
        s_raw = big[n_kn:n_kn + rows] + s_rope
        return jnp.concatenate([s_raw[h * n_q:(h + 1) * n_q, :] * r[h:h + 1, :] for h in range(n_heads)], axis=0)

    def update(s, cb):
        m_old = m_sc[...]
        m_new = jnp.maximum(m_old, jnp.max(s, axis=-1, keepdims=True))
        corr = jnp.exp(m_old - m_new)
        p = jnp.exp(s - m_new)
        l_sc[...] = l_sc[...] * corr + jnp.sum(p, axis=-1, keepdims=True)
        acc_sc[...] = acc_sc[...] * corr + _mm(p.astype(BF16), cb)
        m_sc[...] = m_new

    qr = qp_sc[:, d_nope:d_nope + d_rope].astype(BF16)
    n_chunks = n_pg // pg_chunk
    halves = 2 if n_chunks % 2 == 0 else 1
    per_half = n_chunks // halves
    for hf in range(halves):
        ss, cbs = [], []
        for ck in range(hf * per_half, (hf + 1) * per_half):
            pgs = range(ck * pg_chunk, (ck + 1) * pg_chunk)
            cb = jnp.concatenate([c_refs[p][0, 0] for p in pgs], axis=0).astype(BF16)
            krt = jnp.concatenate([kr_refs[p][0, 0] for p in pgs], axis=1)
            krs = jnp.sum(krt * krt, axis=0, keepdims=True)
            ss.append(scores(cb, _mm(qr, krt.astype(BF16)), krs))
            cbs.append(cb)
        update(jnp.concatenate(ss, axis=1), jnp.concatenate(cbs, axis=0))

    @pl.when(s_idx == pl.num_programs(1) - 1)
    def _():
        cn = cn_ref[0].astype(BF16)
        krn = krn_ref[0]
        kk = krn * krn
        hi = kk.astype(BF16)
        lo = (kk - hi.astype(F32)).astype(BF16)
        ones = jnp.ones((n_heads, kk.shape[1]), BF16)
        krs = _nt(ones, hi) + _nt(ones, lo)
        qpos = lax.broadcasted_iota(jnp.int32, (rows, cn.shape[0]), 0) % n_q
        tpos = lax.broadcasted_iota(jnp.int32, (rows, cn.shape[0]), 1)
        s = scores(cn, _nt(qp_sc[...].astype(BF16), krn.astype(BF16)), krs)
        update(jnp.where(tpos <= qpos, s, NEG), cn)
        lat = (acc_sc[...] / l_sc[...]).astype(BF16)
        for h in range(n_heads):
            sl = slice(h * HEAD_PAD, (h + 1) * HEAD_PAD)
            o_ref[:, sl] = _mm(lat[h * n_q:(h + 1) * n_q, :], wuv_ref[:, sl])


def _sattn(page_table, q, c_new_pad, kr_new_pad, ckv, krope_t, layer, w, dims, n_pg):
    db, n_pages = page_table.shape
    n_q = q.shape[0] // db
    page, kv_lora = ckv.shape[2], ckv.shape[3]
    d_rope = krope_t.shape[2]
    n_heads = dims["n_heads"]
    hp = n_heads * HEAD_PAD
    steps = n_pages // n_pg
    rows = n_heads * n_q
    n_kn = n_heads * dims["d_nope"]
    pg_chunk = 2 if n_pg % 2 == 0 else 1
    body = functools.partial(_sattn_body, n_pg=n_pg, pg_chunk=pg_chunk, n_heads=n_heads, n_q=n_q,
                             d_nope=dims["d_nope"], d_rope=d_rope, d_head=dims["d_head"])
    pt = page_table.reshape(-1)

    def page_map(p):
        return lambda b, s, pt_ref: (layer, pt_ref[b * n_pages + s * n_pg + p], 0, 0)

    fixed = lambda shape: pl.BlockSpec(shape, lambda b, s, pt_ref: (0,) * len(shape))
    grid_spec = pltpu.PrefetchScalarGridSpec(
        num_scalar_prefetch=1,
        grid=(db, steps),
        in_specs=[pl.BlockSpec((n_q, hp), lambda b, s, pt_ref: (b, 0)),
                  pl.BlockSpec((1, page, kv_lora), lambda b, s, pt_ref: (b, 0, 0)),
                  pl.BlockSpec((1, page, HEAD_PAD), lambda b, s, pt_ref: (b, 0, 0)),
                  fixed(w["wukt"].shape), fixed(w["wukh"].shape), fixed(w["knw_raw"].shape), fixed(w["wuv"].shape)]
                 + [pl.BlockSpec((1, 1, page, kv_lora), page_map(p)) for p in range(n_pg)]
                 + [pl.BlockSpec((1, 1, d_rope, page), page_map(p)) for p in range(n_pg)],
        out_specs=pl.BlockSpec((n_q, hp), lambda b, s, pt_ref: (b, 0)),
        scratch_shapes=[pltpu.VMEM((n_kn + rows, kv_lora), BF16), pltpu.VMEM((rows, HEAD_PAD), F32),
                        pltpu.VMEM((rows, 1), F32), pltpu.VMEM((rows, 1), F32), pltpu.VMEM((rows, kv_lora), F32)],
    )
    return pl.pallas_call(
        body,
        grid_spec=grid_spec,
        out_shape=jax.ShapeDtypeStruct((db * n_q, hp), F32),
        compiler_params=_params(dimension_semantics=("parallel", "arbitrary")),
        name="sattn",
    )(pt, q, c_new_pad, kr_new_pad, w["wukt"], w["wukh"], w["knw_raw"], w["wuv"],
      *([ckv] * n_pg), *([krope_t] * n_pg))


def _post_body(x_ref, a_ref, g_ref, pc_ref, wab_ref, wout_ref, h_ref):
    br = _mm(a_ref[...].astype(BF16), wab_ref[...])
    mixed = g_ref[...].astype(F32) * br + pc_ref[...].astype(F32)
    h_ref[...] = x_ref[...] + _mm(mixed.astype(BF16), wout_ref[...])


def _post(x2d, attn, gates, pc, w, tm):
    n, d = x2d.shape
    row = lambda i: (i, 0)
    return pl.pallas_call(
        _post_body,
        grid=(n // tm,),
        in_specs=[pl.BlockSpec((tm, d), row), pl.BlockSpec((tm, attn.shape[1]), row), pl.BlockSpec((tm, d), row),
                  pl.BlockSpec((tm, d), row), _full(w["wab"].shape), _full(w["wout"].shape)],
        out_specs=pl.BlockSpec((tm, d), row),
        out_shape=jax.ShapeDtypeStruct((n, d), F32),
        compiler_params=_params(dimension_semantics=("parallel",)),
        name="post",
    )(x2d, attn, gates, pc, w["wab"], w["wout"])


def _gelu(x):
    return x * (0.5 * (1.0 + jnp.tanh(math.sqrt(2.0 / math.pi) * (x + 0.044715 * (x * x * x)))))


def _top_rows(s, k, with_rank=False):
    rid = lax.broadcasted_iota(jnp.int32, (k, s.shape[1]), 0)

    def step(i, carry):
        work, tops = carry
        m = jnp.max(work, axis=0, keepdims=True)
        tops = jnp.where(rid == i, m, tops)
        work = jnp.where(work == m, -jnp.inf, work)
        return work, tops

    _, tops = lax.fori_loop(0, k, step, (s, jnp.full((k, s.shape[1]), -jnp.inf, F32)))
    if not with_rank:
        return tops
    rank = jnp.zeros(s.shape, F32)
    for b in range(k):
        rank = rank + jnp.where(tops[b:b + 1] > s, 1.0, 0.0)
    return tops, rank


def _peer_body(h_ref, n2_ref, wqt_ref, keys_ref, u_ref, vt_ref, y_ref,
               hn_sc, qt_sc, r1_sc, e1_sc, b_sc, e0_sc, at_sc, g_sc, acc_sc,
               *, n_heads, n_keys, irows, sub):
    s_idx = pl.program_id(1)
    n_blk = pl.num_programs(1) - 1
    k = PEER_TOPK
    t = hn_sc.shape[0]

    @pl.when(s_idx == 0)
    def _():
        hn_sc[...] = _rms(h_ref[...], n2_ref[...]).astype(BF16)
        acc_sc[...] = jnp.zeros(acc_sc.shape, F32)
        at_sc[1] = jnp.zeros(at_sc.shape[1:], F32)
        qrows = qt_sc.shape[0] // 4
        for c in range(4):
            qt_sc[c * qrows:(c + 1) * qrows, :] = _nt(wqt_ref[c * qrows:(c + 1) * qrows, :], hn_sc[...]).astype(BF16)

        def head(h, _):
            def scores(p):
                r0 = pl.multiple_of((h * 2 + p) * n_keys, n_keys)
                return _mm(keys_ref[h * 2 + p], qt_sc[pl.ds(r0, n_keys), :])

            s0 = scores(0)
            s1 = scores(1)
            top0 = _top_rows(s0, k)
            top1, rank1 = _top_rows(s1, k, with_rank=True)
            cand = jnp.concatenate(
                [top0[0:1] + top1]
                + [top0[a:a + 1] + top1[0:8] for a in range(1, 8)]
                + [top0[8:16] + top1[0:1]], axis=0)
            ctop = _top_rows(cand, k + 1)
            c_k = ctop[k - 1:k]
            tau = c_k + 0.5 * (ctop[k:k + 1] - c_k)
            m = top0[0:1] + top1[0:1]
            z = jnp.sum(jnp.where(cand >= c_k, jnp.exp(cand - m), 0.0), axis=0, keepdims=True)
            in0 = s0 >= top0[k - 1:k]
            cnt = jnp.zeros(s0.shape, F32)
            for b in range(k):
                cnt = cnt + jnp.where(s0 + top1[b:b + 1] >= tau, 1.0, 0.0)
            b_sc[h] = jnp.where(in0, cnt, 0.0)
            e0_sc[h] = jnp.where(in0, jnp.exp(s0 - top0[0:1]) / z, 0.0)
            e1_sc[h] = jnp.exp(s1 - top1[0:1])
            r1_sc[h] = rank1
            return 0

        lax.fori_loop(0, n_heads, head, 0)

    def step(at_w, at_r):
        at_w[...] = _gelu(_nt(u_ref[...], hn_sc[...]))
        base = pl.multiple_of(jnp.maximum(s_idx - 1, 0) * irows, irows)
        for sb in range(irows // sub):
            for lt in range(t // LANES):
                ls = slice(lt * LANES, (lt + 1) * LANES)
                b_t = [b_sc[h, pl.ds(base, irows), ls] for h in range(n_heads)]
                e0_t = [e0_sc[h, pl.ds(base, irows), ls] for h in range(n_heads)]
                for ii in range(sb * sub, (sb + 1) * sub):
                    rs = slice(ii * n_keys, (ii + 1) * n_keys)
                    g = at_r[rs, ls]
                    wgt = jnp.zeros((n_keys, LANES), F32)
                    for h in range(n_heads):
                        sel = r1_sc[h, :, ls] < b_t[h][ii:ii + 1, :]
                        wgt = jnp.where(sel, wgt + e1_sc[h, :, ls] * e0_t[h][ii:ii + 1, :], wgt)
                    g_sc[rs, ls] = (wgt * g).astype(BF16)
            cs = slice(sb * sub * n_keys, (sb + 1) * sub * n_keys)
            acc_sc[...] += _mm(vt_ref[0, :, cs], g_sc[cs, :])

    @pl.when(s_idx % 2 == 0)
    def _():
        step(at_sc.at[0], at_sc.at[1])

    @pl.when(s_idx % 2 == 1)
    def _():
        step(at_sc.at[1], at_sc.at[0])

    @pl.when(s_idx == n_blk)
    def _():
        y_ref[...] = h_ref[...] + acc_sc[...].T


def _peer(h2d, w, tt, et):
    n, d = h2d.shape
    n_heads, n_keys = w["n_peer_heads"], w["n_keys"]
    n_exp = w["pu"].shape[0]
    irows = et // n_keys
    n_blk = n_exp // et
    body = functools.partial(_peer_body, n_heads=n_heads, n_keys=n_keys, irows=irows, sub=min(irows, 2))
    small32 = pltpu.VMEM((n_heads, n_keys, tt), F32)
    small16 = pltpu.VMEM((n_heads, n_keys, tt), BF16)
    return pl.pallas_call(
        body,
        grid=(n // tt, n_blk + 1),
        in_specs=[pl.BlockSpec((tt, d), lambda i, s: (i, 0)), _full(w["n2"].shape), _full(w["wqt"].shape),
                  _full(w["keys"].shape),
                  pl.BlockSpec((et, d), lambda i, s: (jnp.minimum(s, n_blk - 1), 0)),
                  pl.BlockSpec((1, d, et), lambda i, s: (jnp.maximum(s - 1, 0), 0, 0))],
        out_specs=pl.BlockSpec((tt, d), lambda i, s: (i, 0)),
        out_shape=jax.ShapeDtypeStruct((n, d), F32),
        scratch_shapes=[pltpu.VMEM((tt, d), BF16), pltpu.VMEM((w["wqt"].shape[0], tt), BF16),
                        small32, small32, small32, small32,
                        pltpu.VMEM((2, et, tt), F32), pltpu.VMEM((et, tt), BF16), pltpu.VMEM((d, tt), F32)],
        compiler_params=_params(dimension_semantics=("parallel", "arbitrary")),
        name="peer",
    )(h2d, w["n2"], w["wqt"], w["keys"], w["pu"], w["pvt"])


def _rot_cols(wr):
    half = wr.shape[-1] // 2
    return jnp.concatenate([-wr[..., half:], wr[..., :half]], axis=-1)


def _rope_tables(pos, d_nope, d_rope):
    half = d_rope // 2
    inv_freq = ROPE_THETA ** (-jnp.arange(half, dtype=F32) / half)
    ang = pos.astype(F32)[:, None] * inv_freq[None, :]
    cos = jnp.cos(ang)
    sin = jnp.sin(ang)
    n = pos.shape[0]
    pad = jnp.zeros((n, HEAD_PAD - d_nope - d_rope), F32)
    ta = jnp.concatenate([jnp.ones((n, d_nope), F32), cos, cos, pad], axis=-1)
    tb = jnp.concatenate([jnp.zeros((n, d_nope), F32), sin, sin, pad], axis=-1)
    return ta, tb


def _layer_weights(l, norm1_w, w_in, q_lat_norm_w, w_q_up, kv_lat_norm_w, w_uk, w_uv, q_norm_w, k_norm_w,
                   w_attn_br, pool_lin_w, pool_scale, w_pool_br, w_out, norm2_w, peer_w_q, peer_keys,
                   peer_u, peer_v, d_rope):
    d = w_in.shape[1]
    q_lora = w_q_up.shape[1]
    n_heads, d_head = w_q_up.shape[2], w_q_up.shape[3]
    kv_lora = w_uk.shape[1]
    d_nope = w_uk.shape[3]
    d_v = w_uv.shape[3]
    off_kv = q_lora
    off_kr = off_kv + kv_lora
    off_pool = off_kr + d_rope
    scale = d_head ** -0.5
    wi = w_in[l]
    w_kr = wi[:, off_kr:off_pool]
    krblk = jnp.concatenate([jnp.zeros((d, d_nope), F32), w_kr, _rot_cols(w_kr)], axis=-1)
    win = jnp.concatenate([wi[:, :off_kr], krblk, wi[:, off_pool:]], axis=-1).astype(BF16)
    wq = w_q_up[l]
    wq = jnp.concatenate([wq[..., :d_nope], wq[..., d_nope:], _rot_cols(wq[..., d_nope:])], axis=-1)
    wq = wq.reshape(q_lora, n_heads * HEAD_PAD).astype(BF16)
    padk = lambda a, width: jnp.concatenate([a, jnp.zeros(a.shape[:-1] + (HEAD_PAD - width,), F32)], axis=-1)
    wuk_pad = padk(w_uk[l], d_nope)
    wuv_pad = padk(w_uv[l], d_v)
    wab = jnp.concatenate([w_attn_br[l].reshape(n_heads, d_v, d),
                           jnp.zeros((n_heads, HEAD_PAD - d_v, d), F32)], axis=1).reshape(n_heads * HEAD_PAD, d)
    n_ph, _, n_keys, half = peer_keys.shape[1:]
    return dict(
        n1=norm1_w[l][None, :], win=win, qlw=q_lat_norm_w[l][None, :], wq=wq, kvw=kv_lat_norm_w[l][None, :],
        wuk=wuk_pad.reshape(kv_lora, n_heads * HEAD_PAD).astype(BF16),
        wuv=wuv_pad.reshape(kv_lora, n_heads * HEAD_PAD).astype(BF16),
        wuvt=wuv_pad.reshape(kv_lora, n_heads * HEAD_PAD).T.astype(BF16),
        qnw=padk(q_norm_w[l], d_head)[None, :] * scale, knw=padk(k_norm_w[l], d_head)[None, :],
        knw_raw=padk(k_norm_w[l], d_head)[None, :],
        wukt=jnp.transpose(w_uk[l], (1, 2, 0)).reshape(n_heads * d_nope, kv_lora).astype(BF16),
        wukh=jnp.transpose(wuk_pad, (1, 0, 2)).astype(BF16),
        wab=wab.astype(BF16), wout=w_out[l].astype(BF16),
        pool_lin=pool_lin_w[l].astype(BF16), pool_scale=pool_scale[l][None, :], wpbr=w_pool_br[l].astype(BF16),
        n2=norm2_w[l][None, :], wqt=peer_w_q[l].T.astype(BF16),
        keys=peer_keys[l].reshape(n_ph * 2, n_keys, half).astype(BF16),
        pu=peer_u[l].astype(BF16),
        n_peer_heads=n_ph, n_keys=n_keys,
    ), dict(q_lora=q_lora, kv_lora=kv_lora, pool_dim=off_pool_dim(wi, off_pool, d), n_heads=n_heads,
            d_head=d_head, d_nope=d_nope, d_v=d_v)


def off_pool_dim(wi, off_pool, d):
    return wi.shape[1] - off_pool - 2 * d


def _tile(n, pref):
    t = pref
    while n % t:
        t //= 2
    return t


def kernel(x_prompt, x_sample, cache_ckv, cache_krope, state_pool, page_table, norm1_w, w_in, q_lat_norm_w, w_q_up, kv_lat_norm_w, w_uk, w_uv, q_norm_w, k_norm_w, w_attn_br, pool_lin_w, pool_scale, w_pool_br, w_out, norm2_w, peer_w_q, peer_keys, peer_u, peer_v):
    batch, seq, d = x_prompt.shape
    db, n_q, _ = x_sample.shape
    depth = w_in.shape[0]
    page = cache_ckv.shape[2]
    d_rope = cache_krope.shape[3]
    n_pages = page_table.shape[1]
    past_len = n_pages * page
    n_state = state_pool.shape[2]
    n_p, n_s = batch * seq, db * n_q
    xp = x_prompt.reshape(n_p, d)
    xs = x_sample.reshape(n_s, d)
    outs = [[] for _ in range(6)]
    for l in range(depth):
        w, dims = _layer_weights(l, norm1_w, w_in, q_lat_norm_w, w_q_up, kv_lat_norm_w, w_uk, w_uv, q_norm_w,
                                 k_norm_w, w_attn_br, pool_lin_w, pool_scale, w_pool_br, w_out, norm2_w,
                                 peer_w_q, peer_keys, peer_u, peer_v, d_rope)
        d_nope = dims["d_nope"]
        kv_lora = dims["kv_lora"]
        pool_dim = dims["pool_dim"]
        rope_sl = slice(d_nope, d_nope + d_rope)
        tm_p = _tile(seq, 256)
        tm_s = _tile(n_s, 256)
        tt_p = _tile(n_p, 512)
        tt_s = _tile(n_s, 512)
        et = _tile(w["pu"].shape[0], 1024)
        w["pvt"] = jnp.swapaxes(peer_v[l].reshape(-1, et, d), 1, 2).astype(BF16)

        ta, tb = _rope_tables(jnp.arange(seq), d_nope, d_rope)
        q, k, v, c, krp, u, gates = _proj(xp, ta, tb, w, dims, tm_p)
        attn = _flash(q, k, v, batch, seq, dims["n_heads"], _tile(seq, 512), _tile(dims["n_heads"], 4),
                      dims["d_v"])
        pc = _pool_prompt(u, gates, w, batch, seq, tm_p)
        h_p = _post(xp, attn, gates, pc, w, tt_p)
        outs[0].append(c.reshape(batch, seq, kv_lora))
        outs[1].append(krp[:, rope_sl].reshape(batch, seq, d_rope))
        outs[2].append(u.reshape(batch, seq, pool_dim)[:, seq - n_state:])

        pos_s = past_len + jnp.arange(n_q)
        ta, tb = _rope_tables(jnp.tile(pos_s, tm_s // n_q), d_nope, d_rope)
        q, _, _, c, krp, u, gates = _proj(xs, ta, tb, w, dims, tm_s)
        zrow = lambda a, rows: jnp.concatenate(
            [a, jnp.zeros((a.shape[0], rows - a.shape[1], a.shape[2]), a.dtype)], axis=1)
        c_new = zrow(c.reshape(db, n_q, kv_lora), page)
        kr_new = zrow(krp.reshape(db, n_q, HEAD_PAD), page)
        n_pg = _tile(n_pages, 16)
        krope_t = jnp.swapaxes(cache_krope, 2, 3)
        attn = _sattn(page_table, q.astype(F32), c_new, kr_new, cache_ckv, krope_t, l, w, dims, n_pg)
        u3 = u.reshape(db, n_q, pool_dim)
        u_full = jnp.concatenate([state_pool[l], u3], axis=1)
        rows = -(-(n_state + n_q + 1) // 8) * 8
        ext = jnp.concatenate([jnp.zeros((db, rows - n_state - n_q, pool_dim), F32), u_full], axis=1)
        pc = _pool_sample(ext.reshape(db * rows, pool_dim), gates, w, db, rows, n_q)
        h_s = _post(xs, attn, gates, pc, w, tt_s)
        xp = _peer(h_p, w, tt_p, et)
        xs = _peer(h_s, w, tt_s, et)
        outs[3].append(c.reshape(db, n_q, kv_lora))
        outs[4].append(krp[:, rope_sl].reshape(db, n_q, d_rope))
        outs[5].append(u_full[:, -n_state:])
    return (xp.reshape(batch, seq, d), xs.reshape(db, n_q, d)) + tuple(jnp.stack(o) for o in outs)
```

```python
import functools
import math

import jax
import jax.numpy as jnp
from jax import lax
from jax.experimental import pallas as pl
from jax.experimental.pallas import tpu as pltpu

F32 = jnp.float32
BF16 = jnp.bfloat16

EPS = 1e-6
ROPE_THETA = 10000.0
POOL_WINDOWS = (2, 4, 8, 16)
PEER_TOPK = 16
NEG = -1e30
LANES = 128
HEAD_PAD = 128
VMEM_LIMIT = 56 * 1024 * 1024


def _nt(a, b):
    return lax.dot_general(a, b, (((1,), (1,)), ((), ())), preferred_element_type=F32)


def _mm(a, b):
    return jnp.dot(a, b, preferred_element_type=F32)


def _rms(x, w):
    return x * lax.rsqrt(jnp.mean(x * x, axis=-1, keepdims=True) + EPS) * w


def _params(**kw):
    return pltpu.CompilerParams(vmem_limit_bytes=VMEM_LIMIT, **kw)


def _full(shape):
    n = len(shape)
    return pl.BlockSpec(shape, lambda *_: (0,) * n)


def _head_norm(blk, ta, tb, w, d_head):
    rot = pltpu.roll(blk, HEAD_PAD - 32, axis=1)
    hq = blk * ta + rot * tb
    ssq = jnp.sum(hq * hq, axis=-1, keepdims=True)
    return hq * lax.rsqrt(ssq * (1.0 / d_head) + EPS) * w


def _proj_body(x_ref, ta_ref, tb_ref, n1_ref, win_ref, qlw_ref, wq_ref, kvw_ref, wuk_ref, wuv_ref,
               qnw_ref, knw_ref, q_ref, k_ref, v_ref, c_ref, kr_ref, u_ref, g_ref,
               *, q_lora, kv_lora, pool_dim, n_heads, d_head, d_v):
    x = x_ref[...]
    xn = _rms(x, n1_ref[...]).astype(BF16)
    proj = _mm(xn, win_ref[...])
    o1 = q_lora
    o2 = o1 + kv_lora
    o3 = o2 + HEAD_PAD
    o4 = o3 + pool_dim
    ta = ta_ref[...]
    tb = tb_ref[...]

    ql = _rms(proj[:, :o1], qlw_ref[...]).astype(BF16)
    qu = _mm(ql, wq_ref[...])
    c = _rms(proj[:, o1:o2], kvw_ref[...])
    c_ref[...] = c
    cb = c.astype(BF16)
    krb = proj[:, o2:o3]
    krp = krb * ta + pltpu.roll(krb, HEAD_PAD - 32, axis=1) * tb
    kr_ref[...] = krp
    ku = _mm(cb, wuk_ref[...])
    vt = _nt(wuv_ref[...], cb)
    vrow = lax.broadcasted_iota(jnp.int32, vt.shape, 0) % HEAD_PAD
    v_ref[0] = jnp.where(vrow == d_v, 1.0, vt).astype(BF16)
    qnw = qnw_ref[...]
    knw = knw_ref[...]
    for h in range(n_heads):
        sl = slice(h * HEAD_PAD, (h + 1) * HEAD_PAD)
        q_ref[:, sl] = _head_norm(qu[:, sl], ta, tb, qnw, d_head).astype(BF16)
        kh = ku[:, sl] + krp
        ssq = jnp.sum(kh * kh, axis=-1, keepdims=True)
        k_ref[:, sl] = (kh * lax.rsqrt(ssq * (1.0 / d_head) + EPS) * knw).astype(BF16)
    u_ref[...] = proj[:, o3:o4]
    g_ref[...] = jax.nn.sigmoid(proj[:, o4:]).astype(BF16)


def _proj(x2d, ta, tb, w, dims, tm):
    n, d = x2d.shape
    nt = n // tm
    tab_blocks = ta.shape[0] // tm
    row = lambda i: (i, 0)
    tab = lambda i: (i % tab_blocks, 0)
    hp = dims["n_heads"] * HEAD_PAD
    body = functools.partial(_proj_body, q_lora=dims["q_lora"], kv_lora=dims["kv_lora"],
                             pool_dim=dims["pool_dim"], n_heads=dims["n_heads"], d_head=dims["d_head"],
                             d_v=dims["d_v"])
    consts = [w["n1"], w["win"], w["qlw"], w["wq"], w["kvw"], w["wuk"], w["wuvt"], w["qnw"], w["knw"]]
    return pl.pallas_call(
        body,
        grid=(nt,),
        in_specs=[pl.BlockSpec((tm, d), row), pl.BlockSpec((tm, HEAD_PAD), tab), pl.BlockSpec((tm, HEAD_PAD), tab)]
                 + [_full(a.shape) for a in consts],
        out_specs=[pl.BlockSpec((tm, hp), row), pl.BlockSpec((tm, hp), row),
                   pl.BlockSpec((1, hp, tm), lambda i: (i, 0, 0)),
                   pl.BlockSpec((tm, dims["kv_lora"]), row), pl.BlockSpec((tm, HEAD_PAD), row),
                   pl.BlockSpec((tm, dims["pool_dim"]), row), pl.BlockSpec((tm, 2 * d), row)],
        out_shape=[jax.ShapeDtypeStruct((n, hp), BF16), jax.ShapeDtypeStruct((n, hp), BF16),
                   jax.ShapeDtypeStruct((nt, hp, tm), BF16), jax.ShapeDtypeStruct((n, dims["kv_lora"]), F32),
                   jax.ShapeDtypeStruct((n, HEAD_PAD), F32), jax.ShapeDtypeStruct((n, dims["pool_dim"]), F32),
                   jax.ShapeDtypeStruct((n, 2 * d), BF16)],
        compiler_params=_params(dimension_semantics=("parallel",)),
        name="proj",
    )(x2d, ta, tb, *consts)


def _window_sums(ext):
    s2 = ext + pltpu.roll(ext, 1, axis=0)
    s4 = s2 + pltpu.roll(s2, 2, axis=0)
    s8 = s4 + pltpu.roll(s4, 4, axis=0)
    s16 = s8 + pltpu.roll(s8, 8, axis=0)
    return (s2, s4, s8, s16)


def _pool_tail(z_groups, g2, lin_ref, scale_ref, wbr_ref, out_ref):
    zl = [_mm(z.astype(BF16), lin_ref[g]) for g, z in enumerate(z_groups)]
    pool = jnp.concatenate(zl, axis=-1) * scale_ref[...]
    br = _mm(pool.astype(BF16), wbr_ref[...])
    out_ref[...] = (g2.astype(F32) * br).astype(BF16)


def _pool_prompt_body(u_ref, halo_ref, g_ref, lin_ref, scale_ref, wbr_ref, out_ref, *, tm, gd):
    j = pl.program_id(1)
    u = u_ref[...]
    prev = jnp.where(j == 0, 0.0, halo_ref[...])
    sums = _window_sums(jnp.concatenate([prev, u], axis=0))
    pos = j * tm + lax.broadcasted_iota(jnp.int32, (tm, gd), 0)
    zs = []
    for g, wdw in enumerate(POOL_WINDOWS):
        sl = slice(g * gd, (g + 1) * gd)
        cnt = jnp.minimum(pos + 1, wdw).astype(F32)
        zs.append(sums[g][16:, sl] / cnt - u[:, sl])
    _pool_tail(zs, g_ref[...], lin_ref, scale_ref, wbr_ref, out_ref)


def _pool_prompt(u2d, gates, w, batch, seq, tm):
    n, pd = u2d.shape
    d = gates.shape[1] // 2
    ns = seq // tm
    gd = pd // len(POOL_WINDOWS)
    hb = tm // 16
    body = functools.partial(_pool_prompt_body, tm=tm, gd=gd)
    return pl.pallas_call(
        body,
        grid=(batch, ns),
        in_specs=[pl.BlockSpec((tm, pd), lambda b, j: (b * ns + j, 0)),
                  pl.BlockSpec((16, pd), lambda b, j: (jnp.maximum((b * ns + j) * hb - 1, 0), 0)),
                  pl.BlockSpec((tm, d), lambda b, j: (b * ns + j, 1)),
                  _full(w["pool_lin"].shape), _full(w["pool_scale"].shape), _full(w["wpbr"].shape)],
        out_specs=pl.BlockSpec((tm, d), lambda b, j: (b * ns + j, 0)),
        out_shape=jax.ShapeDtypeStruct((n, d), BF16),
        compiler_params=_params(dimension_semantics=("parallel", "parallel")),
        name="pool_prompt",
    )(u2d, u2d, gates, w["pool_lin"], w["pool_scale"], w["wpbr"])


def _pool_sample_body(ext_ref, g_ref, lin_ref, scale_ref, wbr_ref, out_ref, *, nb, rows, n_new, gd):
    ext = ext_ref[...]
    sums = _window_sums(ext)
    pick = lambda a: a.reshape(nb, rows, a.shape[-1])[:, rows - n_new:, :].reshape(nb * n_new, a.shape[-1])
    zs = []
    for g, wdw in enumerate(POOL_WINDOWS):
        sl = slice(g * gd, (g + 1) * gd)
        zs.append(pick(sums[g][:, sl]) / float(wdw) - pick(ext[:, sl]))
    _pool_tail(zs, g_ref[...], lin_ref, scale_ref, wbr_ref, out_ref)


def _pool_sample(ext2d, gates, w, db, rows, n_new):
    pd = ext2d.shape[1]
    d = gates.shape[1] // 2
    gd = pd // len(POOL_WINDOWS)
    nb = _tile(db, 16)
    body = functools.partial(_pool_sample_body, nb=nb, rows=rows, n_new=n_new, gd=gd)
    return pl.pallas_call(
        body,
        grid=(db // nb,),
        in_specs=[pl.BlockSpec((nb * rows, pd), lambda i: (i, 0)), pl.BlockSpec((nb * n_new, d), lambda i: (i, 1)),
                  _full(w["pool_lin"].shape), _full(w["pool_scale"].shape), _full(w["wpbr"].shape)],
        out_specs=pl.BlockSpec((nb * n_new, d), lambda i: (i, 0)),
        out_shape=jax.ShapeDtypeStruct((db * n_new, d), BF16),
        compiler_params=_params(dimension_semantics=("parallel",)),
        name="pool_sample",
    )(ext2d, gates, w["pool_lin"], w["pool_scale"], w["wpbr"])


def _flash_body(q_ref, k_ref, vt_ref, o_ref, m_sc, acc_sc, *, tq, hps, l_row):
    i = pl.program_id(2)
    tv = vt_ref.shape[2]
    m_sc[...] = jnp.full(m_sc.shape, NEG, F32)
    acc_sc[...] = jnp.zeros(acc_sc.shape, F32)

    def chunk(j, masked):
        r0 = pl.multiple_of(j * tq, tq)
        for hh in range(hps):
            sl = slice(hh * HEAD_PAD, (hh + 1) * HEAD_PAD)
            st = _nt(k_ref[pl.ds(r0, tq), sl], q_ref[:, sl])
            if masked:
                kpos = lax.broadcasted_iota(jnp.int32, st.shape, 0)
                qpos = lax.broadcasted_iota(jnp.int32, st.shape, 1)
                st = jnp.where(kpos <= qpos, st, NEG)
            m_old = m_sc[hh]
            m_new = jnp.maximum(m_old, jnp.max(st, axis=0, keepdims=True))
            p = jnp.exp(st - m_new).astype(BF16)
            pv = _mm(vt_ref[j * (tq // tv), sl, :], p[0:tv])
            for c in range(1, tq // tv):
                pv = pv + _mm(vt_ref[j * (tq // tv) + c, sl, :], p[c * tv:(c + 1) * tv])
            acc_sc[hh] = acc_sc[hh] * jnp.exp(m_old - m_new) + pv
            m_sc[hh] = m_new

    def full_chunk(j, carry):
        chunk(j, False)
        return carry

    lax.fori_loop(0, i, full_chunk, 0)
    chunk(i, True)
    for hh in range(hps):
        acc = acc_sc[hh]
        o_ref[:, hh * HEAD_PAD:(hh + 1) * HEAD_PAD] = (acc / acc[l_row:l_row + 1, :]).T.astype(BF16)


def _flash(q, k, vt, batch, seq, n_heads, tq, hps, l_row):
    n = q.shape[0]
    nq = seq // tq
    w = hps * HEAD_PAD
    tv = vt.shape[2]
    body = functools.partial(_flash_body, tq=tq, hps=hps, l_row=l_row)
    return pl.pallas_call(
        body,
        grid=(batch, n_heads // hps, nq),
        in_specs=[pl.BlockSpec((tq, w), lambda b, h, i: (b * nq + i, h)),
                  pl.BlockSpec((seq, w), lambda b, h, i: (b, h)),
                  pl.BlockSpec((seq // tv, w, tv), lambda b, h, i: (b, h, 0))],
        out_specs=pl.BlockSpec((tq, w), lambda b, h, i: (b * nq + i, h)),
        out_shape=jax.ShapeDtypeStruct((n, n_heads * HEAD_PAD), BF16),
        scratch_shapes=[pltpu.VMEM((hps, 1, tq), F32), pltpu.VMEM((hps, HEAD_PAD, tq), F32)],
        compiler_params=_params(dimension_semantics=("parallel", "parallel", "arbitrary")),
        name="flash",
    )(q, k, vt)


def _sattn_body(pt_ref, q_ref, cn_ref, krn_ref, wukt_ref, wukh_ref, knw_ref, wuv_ref, *rest,
                n_pg, pg_chunk, n_heads, n_q, d_nope, d_rope, d_head):
    c_refs = rest[:n_pg]
    kr_refs = rest[n_pg:2 * n_pg]
    o_ref = rest[2 * n_pg]
    wall_sc, qp_sc, m_sc, l_sc, acc_sc = rest[2 * n_pg + 1:]
    s_idx = pl.program_id(1)
    n_kn = n_heads * d_nope
    rows = n_heads * n_q

    @pl.when(s_idx == 0)
    def _():
        knw = knw_ref[...]
        qcs = []
        for h in range(n_heads):
            qp = q_ref[:, h * HEAD_PAD:(h + 1) * HEAD_PAD] * knw
            qp_sc[h * n_q:(h + 1) * n_q, :] = qp
            qcs.append(_nt(qp.astype(BF16), wukh_ref[h]))
        wall_sc[0:n_kn, :] = wukt_ref[...]
        wall_sc[n_kn:n_kn + rows, :] = jnp.concatenate(qcs, axis=0).astype(BF16)
        m_sc[...] = jnp.full(m_sc.shape, NEG, F32)
        l_sc[...] = jnp.zeros(l_sc.shape, F32)
        acc_sc[...] = jnp.zeros(acc_sc.shape, F32)

    def scores(cb, s_rope, krs):
        big = _nt(wall_sc[...], cb)
        knt = big[0:n_kn]
        sq = knt * knt
        ssq = jnp.sum(sq.reshape(n_heads, d_nope, sq.shape[-1]), axis=1) + krs
        r = lax.rsqrt(ssq * (1.0 / d_head) + EPS)
        s_raw = big[n_kn:n_kn + rows] + s_rope
        return jnp.concatenate([s_raw[h * n_q:(h + 1) * n_q, :] * r[h:h + 1, :] for h in range(n_heads)], axis=0)

    def update(s, cb):
        m_old = m_sc[...]
        m_new = jnp.maximum(m_old, jnp.max(s, axis=-1, keepdims=True))
        corr = jnp.exp(m_old - m_new)
        p = jnp.exp(s - m_new)
        l_sc[...] = l_sc[...] * corr + jnp.sum(p, axis=-1, keepdims=True)
        acc_sc[...] = acc_sc[...] * corr + _mm(p.astype(BF16), cb)
        m_sc[...] = m_new

    qr = qp_sc[:, d_nope:d_nope + d_rope].astype(BF16)
    n_chunks = n_pg // pg_chunk
    halves = 2 if n_chunks % 2 == 0 else 1
    per_half = n_chunks // halves
    for hf in range(halves):
        ss, cbs = [], []
        for ck in range(hf * per_half, (hf + 1) * per_half):
            pgs = range(ck * pg_chunk, (ck + 1) * pg_chunk)
            cb = jnp.concatenate([c_refs[p][0, 0] for p in pgs], axis=0).astype(BF16)
            krt = jnp.concatenate([kr_refs[p][0, 0] for p in pgs], axis=1)
            krs = jnp.sum(krt * krt, axis=0, keepdims=True)
            ss.append(scores(cb, _mm(qr, krt.astype(BF16)), krs))
            cbs.append(cb)
        update(jnp.concatenate(ss, axis=1), jnp.concatenate(cbs, axis=0))

    @pl.when(s_idx == pl.num_programs(1) - 1)
    def _():
        cn = cn_ref[0].astype(BF16)
        krn = krn_ref[0]
        kk = krn * krn
        hi = kk.astype(BF16)
        lo = (kk - hi.astype(F32)).astype(BF16)
        ones = jnp.ones((n_heads, kk.shape[1]), BF16)
        krs = _nt(ones, hi) + _nt(ones, lo)
        qpos = lax.broadcasted_iota(jnp.int32, (rows, cn.shape[0]), 0) % n_q
        tpos = lax.broadcasted_iota(jnp.int32, (rows, cn.shape[0]), 1)
        s = scores(cn, _nt(qp_sc[...].astype(BF16), krn.astype(BF16)), krs)
        update(jnp.where(tpos <= qpos, s, NEG), cn)
        lat = (acc_sc[...] / l_sc[...]).astype(BF16)
        for h in range(n_heads):
            sl = slice(h * HEAD_PAD, (h + 1) * HEAD_PAD)
            o_ref[:, sl] = _mm(lat[h * n_q:(h + 1) * n_q, :], wuv_ref[:, sl])


def _sattn(page_table, q, c_new_pad, kr_new_pad, ckv, krope_t, layer, w, dims, n_pg):
    db, n_pages = page_table.shape
    n_q = q.shape[0] // db
    page, kv_lora = ckv.shape[2], ckv.shape[3]
    d_rope = krope_t.shape[2]
    n_heads = dims["n_heads"]
    hp = n_heads * HEAD_PAD
    steps = n_pages // n_pg
    rows = n_heads * n_q
    n_kn = n_heads * dims["d_nope"]
    pg_chunk = 2 if n_pg % 2 == 0 else 1
    body = functools.partial(_sattn_body, n_pg=n_pg, pg_chunk=pg_chunk, n_heads=n_heads, n_q=n_q,
                             d_nope=dims["d_nope"], d_rope=d_rope, d_head=dims["d_head"])
    pt = page_table.reshape(-1)

    def page_map(p):
        return lambda b, s, pt_ref: (layer, pt_ref[b * n_pages + s * n_pg + p], 0, 0)

    fixed = lambda shape: pl.BlockSpec(shape, lambda b, s, pt_ref: (0,) * len(shape))
    grid_spec = pltpu.PrefetchScalarGridSpec(
        num_scalar_prefetch=1,
        grid=(db, steps),
        in_specs=[pl.BlockSpec((n_q, hp), lambda b, s, pt_ref: (b, 0)),
                  pl.BlockSpec((1, page, kv_lora), lambda b, s, pt_ref: (b, 0, 0)),
                  pl.BlockSpec((1, page, HEAD_PAD), lambda b, s, pt_ref: (b, 0, 0)),
                  fixed(w["wukt"].shape), fixed(w["wukh"].shape), fixed(w["knw_raw"].shape), fixed(w["wuv"].shape)]
                 + [pl.BlockSpec((1, 1, page, kv_lora), page_map(p)) for p in range(n_pg)]
                 + [pl.BlockSpec((1, 1, d_rope, page), page_map(p)) for p in range(n_pg)],
        out_specs=pl.BlockSpec((n_q, hp), lambda b, s, pt_ref: (b, 0)),
        scratch_shapes=[pltpu.VMEM((n_kn + rows, kv_lora), BF16), pltpu.VMEM((rows, HEAD_PAD), F32),
                        pltpu.VMEM((rows, 1), F32), pltpu.VMEM((rows, 1), F32), pltpu.VMEM((rows, kv_lora), F32)],
    )
    return pl.pallas_call(
        body,
        grid_spec=grid_spec,
        out_shape=jax.ShapeDtypeStruct((db * n_q, hp), F32),
        compiler_params=_params(dimension_semantics=("parallel", "arbitrary")),
        name="sattn",
    )(pt, q, c_new_pad, kr_new_pad, w["wukt"], w["wukh"], w["knw_raw"], w["wuv"],
      *([ckv] * n_pg), *([krope_t] * n_pg))


def _post_body(x_ref, a_ref, g_ref, pc_ref, wab_ref, wout_ref, h_ref):
    br = _mm(a_ref[...].astype(BF16), wab_ref[...])
    mixed = g_ref[...].astype(F32) * br + pc_ref[...].astype(F32)
    h_ref[...] = x_ref[...] + _mm(mixed.astype(BF16), wout_ref[...])


def _post(x2d, attn, gates, pc, w, tm):
    n, d = x2d.shape
    row = lambda i: (i, 0)
    return pl.pallas_call(
        _post_body,
        grid=(n // tm,),
        in_specs=[pl.BlockSpec((tm, d), row), pl.BlockSpec((tm, attn.shape[1]), row), pl.BlockSpec((tm, d), row),
                  pl.BlockSpec((tm, d), row), _full(w["wab"].shape), _full(w["wout"].shape)],
        out_specs=pl.BlockSpec((tm, d), row),
        out_shape=jax.ShapeDtypeStruct((n, d), F32),
        compiler_params=_params(dimension_semantics=("parallel",)),
        name="post",
    )(x2d, attn, gates, pc, w["wab"], w["wout"])


def _gelu(x):
    k = -2.0 * math.sqrt(2.0 / math.pi) * math.log2(math.e)
    return x / (1.0 + jnp.exp2(x * (k + (k * 0.044715) * (x * x))))


def _batcher_pairs(n):
    pairs = []
    p = 1
    while p < n:
        k = p
        while k >= 1:
            for j in range(k % p, n - k, 2 * k):
                for i in range(min(k, n - j - k)):
                    if (i + j) // (2 * p) == (i + j + k) // (2 * p):
                        pairs.append((i + j, i + j + k))
            k //= 2
        p *= 2
    return pairs


def _top_rows(s, k, with_rank=False):
    r, t = s.shape
    groups = r // 8
    st = [s[8 * v:8 * v + 8] for v in range(groups)]
    for a, b in _batcher_pairs(1 << (groups - 1).bit_length()):
        if b < groups:
            st[a], st[b] = jnp.maximum(st[a], st[b]), jnp.minimum(st[a], st[b])
    rid = lax.broadcasted_iota(jnp.int32, (k, t), 0)
    ninf = jnp.full((1, 8, t), -jnp.inf, F32)

    def step(i, carry):
        stack, tops = carry
        m = jnp.max(stack[0], axis=0, keepdims=True)
        tops = jnp.where(rid == i, m, tops)
        hit = stack[0] == m
        stack = jnp.where(hit[None], jnp.concatenate([stack[1:], ninf], axis=0), stack)
        return stack, tops

    _, tops = lax.fori_loop(0, k, step, (jnp.stack(st), jnp.full((k, t), -jnp.inf, F32)))
    if not with_rank:
        return tops
    rank = jnp.zeros(s.shape, F32)
    for b in range(k):
        rank = rank + jnp.where(tops[b:b + 1] > s, 1.0, 0.0)
    return tops, rank


def _peer_body(h_ref, n2_ref, wqt_ref, keys_ref, u_ref, vt_ref, y_ref,
               hn_sc, qt_sc, r1_sc, e1_sc, b_sc, e0_sc, at_sc, g_sc, acc_sc,
               *, n_heads, n_keys, irows, sub):
    s_idx = pl.program_id(1)
    n_blk = pl.num_programs(1) - 1
    k = PEER_TOPK
    t = hn_sc.shape[0]

    @pl.when(s_idx == 0)
    def _():
        hn_sc[...] = _rms(h_ref[...], n2_ref[...]).astype(BF16)
        acc_sc[...] = jnp.zeros(acc_sc.shape, F32)
        at_sc[1] = jnp.zeros(at_sc.shape[1:], BF16)
        qrows = qt_sc.shape[0] // 4
        for c in range(4):
            qt_sc[c * qrows:(c + 1) * qrows, :] = _nt(wqt_ref[c * qrows:(c + 1) * qrows, :], hn_sc[...]).astype(BF16)

        def head(h, _):
            def scores(p):
                r0 = pl.multiple_of((h * 2 + p) * n_keys, n_keys)
                return _mm(keys_ref[h * 2 + p], qt_sc[pl.ds(r0, n_keys), :])

            s0 = scores(0)
            s1 = scores(1)
            top0 = _top_rows(s0, k)
            top1, rank1 = _top_rows(s1, k, with_rank=True)
            cand = jnp.concatenate(
                [top0[0:1] + top1]
                + [top0[a:a + 1] + top1[0:8] for a in range(1, 8)]
                + [top0[8:16] + top1[0:1]], axis=0)
            ctop = _top_rows(cand, k + 1)
            c_k = ctop[k - 1:k]
            tau = c_k + 0.5 * (ctop[k:k + 1] - c_k)
            m = top0[0:1] + top1[0:1]
            z = jnp.sum(jnp.where(cand >= c_k, jnp.exp(cand - m), 0.0), axis=0, keepdims=True)
            in0 = s0 >= top0[k - 1:k]
            cnt = jnp.zeros(s0.shape, F32)
            for b in range(k):
                cnt = cnt + jnp.where(s0 + top1[b:b + 1] >= tau, 1.0, 0.0)
            b_sc[h] = jnp.where(in0, cnt, 0.0)
            e0_sc[h] = jnp.where(in0, jnp.exp(s0 - top0[0:1]) / z, 0.0)
            e1_sc[h] = jnp.exp(s1 - top1[0:1]).astype(BF16)
            r1_sc[h] = rank1.astype(BF16)
            return 0

        lax.fori_loop(0, n_heads, head, 0)

    def step(at_w, at_r):
        at_w[...] = _gelu(_nt(u_ref[...], hn_sc[...])).astype(BF16)
        base = pl.multiple_of(jnp.maximum(s_idx - 1, 0) * irows, irows)
        for sb in range(irows // sub):
            for lt in range(t // LANES):
                ls = slice(lt * LANES, (lt + 1) * LANES)
                b_t = [b_sc[h, pl.ds(base, irows), ls] for h in range(n_heads)]
                e0_t = [e0_sc[h, pl.ds(base, irows), ls] for h in range(n_heads)]
                for ii in range(sb * sub, (sb + 1) * sub):
                    rs = slice(ii * n_keys, (ii + 1) * n_keys)
                    wgt = jnp.zeros((n_keys, LANES), BF16)
                    for h in range(n_heads):
                        sel = r1_sc[h, :, ls] < b_t[h][ii:ii + 1, :].astype(BF16)
                        wgt = wgt + jnp.where(sel, e1_sc[h, :, ls] * e0_t[h][ii:ii + 1, :].astype(BF16),
                                              jnp.zeros((), BF16))
                    g_sc[rs, ls] = wgt * at_r[rs, ls]
            cs = slice(sb * sub * n_keys, (sb + 1) * sub * n_keys)
            acc_sc[...] += _mm(vt_ref[0, :, cs], g_sc[cs, :])

    step(at_sc.at[s_idx % 2], at_sc.at[(s_idx + 1) % 2])

    @pl.when(s_idx == n_blk)
    def _():
        y_ref[...] = h_ref[...] + acc_sc[...].T


def _peer(h2d, w, tt, et):
    n, d = h2d.shape
    n_heads, n_keys = w["n_peer_heads"], w["n_keys"]
    n_exp = w["pu"].shape[0]
    irows = et // n_keys
    n_blk = n_exp // et
    body = functools.partial(_peer_body, n_heads=n_heads, n_keys=n_keys, irows=irows, sub=min(irows, 2))
    small32 = pltpu.VMEM((n_heads, n_keys, tt), F32)
    small16 = pltpu.VMEM((n_heads, n_keys, tt), BF16)
    return pl.pallas_call(
        body,
        grid=(n // tt, n_blk + 1),
        in_specs=[pl.BlockSpec((tt, d), lambda i, s: (i, 0)), _full(w["n2"].shape), _full(w["wqt"].shape),
                  _full(w["keys"].shape),
                  pl.BlockSpec((et, d), lambda i, s: (jnp.minimum(s, n_blk - 1), 0)),
                  pl.BlockSpec((1, d, et), lambda i, s: (jnp.maximum(s - 1, 0), 0, 0))],
        out_specs=pl.BlockSpec((tt, d), lambda i, s: (i, 0)),
        out_shape=jax.ShapeDtypeStruct((n, d), F32),
        scratch_shapes=[pltpu.VMEM((tt, d), BF16), pltpu.VMEM((w["wqt"].shape[0], tt), BF16),
                        small16, small16, small32, small32,
                        pltpu.VMEM((2, et, tt), BF16), pltpu.VMEM((et, tt), BF16), pltpu.VMEM((d, tt), F32)],
        compiler_params=_params(dimension_semantics=("parallel", "arbitrary")),
        name="peer",
    )(h2d, w["n2"], w["wqt"], w["keys"], w["pu"], w["pvt"])


def _rot_cols(wr):
    half = wr.shape[-1] // 2
    return jnp.concatenate([-wr[..., half:], wr[..., :half]], axis=-1)


def _rope_tables(pos, d_nope, d_rope):
    half = d_rope // 2
    inv_freq = ROPE_THETA ** (-jnp.arange(half, dtype=F32) / half)
    ang = pos.astype(F32)[:, None] * inv_freq[None, :]
    cos = jnp.cos(ang)
    sin = jnp.sin(ang)
    n = pos.shape[0]
    pad = jnp.zeros((n, HEAD_PAD - d_nope - d_rope), F32)
    ta = jnp.concatenate([jnp.ones((n, d_nope), F32), cos, cos, pad], axis=-1)
    tb = jnp.concatenate([jnp.zeros((n, d_nope), F32), sin, sin, pad], axis=-1)
    return ta, tb


def _layer_weights(l, norm1_w, w_in, q_lat_norm_w, w_q_up, kv_lat_norm_w, w_uk, w_uv, q_norm_w, k_norm_w,
                   w_attn_br, pool_lin_w, pool_scale, w_pool_br, w_out, norm2_w, peer_w_q, peer_keys,
                   peer_u, peer_v, d_rope):
    d = w_in.shape[1]
    q_lora = w_q_up.shape[1]
    n_heads, d_head = w_q_up.shape[2], w_q_up.shape[3]
    kv_lora = w_uk.shape[1]
    d_nope = w_uk.shape[3]
    d_v = w_uv.shape[3]
    off_kv = q_lora
    off_kr = off_kv + kv_lora
    off_pool = off_kr + d_rope
    scale = d_head ** -0.5
    wi = w_in[l]
    w_kr = wi[:, off_kr:off_pool]
    krblk = jnp.concatenate([jnp.zeros((d, d_nope), F32), w_kr, _rot_cols(w_kr)], axis=-1)
    win = jnp.concatenate([wi[:, :off_kr], krblk, wi[:, off_pool:]], axis=-1).astype(BF16)
    wq = w_q_up[l]
    wq = jnp.concatenate([wq[..., :d_nope], wq[..., d_nope:], _rot_cols(wq[..., d_nope:])], axis=-1)
    wq = wq.reshape(q_lora, n_heads * HEAD_PAD).astype(BF16)
    padk = lambda a, width: jnp.concatenate([a, jnp.zeros(a.shape[:-1] + (HEAD_PAD - width,), F32)], axis=-1)
    wuk_pad = padk(w_uk[l], d_nope)
    wuv_pad = padk(w_uv[l], d_v)
    wab = jnp.concatenate([w_attn_br[l].reshape(n_heads, d_v, d),
                           jnp.zeros((n_heads, HEAD_PAD - d_v, d), F32)], axis=1).reshape(n_heads * HEAD_PAD, d)
    n_ph, _, n_keys, half = peer_keys.shape[1:]
    return dict(
        n1=norm1_w[l][None, :], win=win, qlw=q_lat_norm_w[l][None, :], wq=wq, kvw=kv_lat_norm_w[l][None, :],
        wuk=wuk_pad.reshape(kv_lora, n_heads * HEAD_PAD).astype(BF16),
        wuv=wuv_pad.reshape(kv_lora, n_heads * HEAD_PAD).astype(BF16),
        wuvt=wuv_pad.reshape(kv_lora, n_heads * HEAD_PAD).T.astype(BF16),
        qnw=padk(q_norm_w[l], d_head)[None, :] * scale, knw=padk(k_norm_w[l], d_head)[None, :],
        knw_raw=padk(k_norm_w[l], d_head)[None, :],
        wukt=jnp.transpose(w_uk[l], (1, 2, 0)).reshape(n_heads * d_nope, kv_lora).astype(BF16),
        wukh=jnp.transpose(wuk_pad, (1, 0, 2)).astype(BF16),
        wab=wab.astype(BF16), wout=w_out[l].astype(BF16),
        pool_lin=pool_lin_w[l].astype(BF16), pool_scale=pool_scale[l][None, :], wpbr=w_pool_br[l].astype(BF16),
        n2=norm2_w[l][None, :], wqt=peer_w_q[l].T.astype(BF16),
        keys=peer_keys[l].reshape(n_ph * 2, n_keys, half).astype(BF16),
        pu=peer_u[l].astype(BF16),
        n_peer_heads=n_ph, n_keys=n_keys,
    ), dict(q_lora=q_lora, kv_lora=kv_lora, pool_dim=off_pool_dim(wi, off_pool, d), n_heads=n_heads,
            d_head=d_head, d_nope=d_nope, d_v=d_v)


def off_pool_dim(wi, off_pool, d):
    return wi.shape[1] - off_pool - 2 * d


def _tile(n, pref):
    t = pref
    while n % t:
        t //= 2
    return t


def kernel(x_prompt, x_sample, cache_ckv, cache_krope, state_pool, page_table, norm1_w, w_in, q_lat_norm_w, w_q_up, kv_lat_norm_w, w_uk, w_uv, q_norm_w, k_norm_w, w_attn_br, pool_lin_w, pool_scale, w_pool_br, w_out, norm2_w, peer_w_q, peer_keys, peer_u, peer_v):
    batch, seq, d = x_prompt.shape
    db, n_q, _ = x_sample.shape
    depth = w_in.shape[0]
    page = cache_ckv.shape[2]
    d_rope = cache_krope.shape[3]
    n_pages = page_table.shape[1]
    past_len = n_pages * page
    n_state = state_pool.shape[2]
    n_p, n_s = batch * seq, db * n_q
    xp = x_prompt.reshape(n_p, d)
    xs = x_sample.reshape(n_s, d)
    outs = [[] for _ in range(6)]
    for l in range(depth):
        w, dims = _layer_weights(l, norm1_w, w_in, q_lat_norm_w, w_q_up, kv_lat_norm_w, w_uk, w_uv, q_norm_w,
                                 k_norm_w, w_attn_br, pool_lin_w, pool_scale, w_pool_br, w_out, norm2_w,
                                 peer_w_q, peer_keys, peer_u, peer_v, d_rope)
        d_nope = dims["d_nope"]
        kv_lora = dims["kv_lora"]
        pool_dim = dims["pool_dim"]
        rope_sl = slice(d_nope, d_nope + d_rope)
        tm_p = _tile(seq, 256)
        tm_s = _tile(n_s, 256)
        tt_p = _tile(n_p, 512)
        tt_s = _tile(n_s, 512)
        et = _tile(w["pu"].shape[0], 1024)
        w["pvt"] = jnp.swapaxes(peer_v[l].reshape(-1, et, d), 1, 2).astype(BF16)

        ta, tb = _rope_tables(jnp.arange(seq), d_nope, d_rope)
        q, k, v, c, krp, u, gates = _proj(xp, ta, tb, w, dims, tm_p)
        attn = _flash(q, k, v, batch, seq, dims["n_heads"], _tile(seq, 512), _tile(dims["n_heads"], 4),
                      dims["d_v"])
        pc = _pool_prompt(u, gates, w, batch, seq, tm_p)
        h_p = _post(xp, attn, gates, pc, w, tt_p)
        outs[0].append(c.reshape(batch, seq, kv_lora))
        outs[1].append(krp[:, rope_sl].reshape(batch, seq, d_rope))
        outs[2].append(u.reshape(batch, seq, pool_dim)[:, seq - n_state:])

        pos_s = past_len + jnp.arange(n_q)
        ta, tb = _rope_tables(jnp.tile(pos_s, tm_s // n_q), d_nope, d_rope)
        q, _, _, c, krp, u, gates = _proj(xs, ta, tb, w, dims, tm_s)
        zrow = lambda a, rows: jnp.concatenate(
            [a, jnp.zeros((a.shape[0], rows - a.shape[1], a.shape[2]), a.dtype)], axis=1)
        c_new = zrow(c.reshape(db, n_q, kv_lora), page)
        kr_new = zrow(krp.reshape(db, n_q, HEAD_PAD), page)
        n_pg = _tile(n_pages, 16)
        krope_t = jnp.swapaxes(cache_krope, 2, 3)
        attn = _sattn(page_table, q.astype(F32), c_new, kr_new, cache_ckv, krope_t, l, w, dims, n_pg)
        u3 = u.reshape(db, n_q, pool_dim)
        u_full = jnp.concatenate([state_pool[l], u3], axis=1)
        rows = -(-(n_state + n_q + 1) // 8) * 8
        ext = jnp.concatenate([jnp.zeros((db, rows - n_state - n_q, pool_dim), F32), u_full], axis=1)
        pc = _pool_sample(ext.reshape(db * rows, pool_dim), gates, w, db, rows, n_q)
        h_s = _post(xs, attn, gates, pc, w, tt_s)
        xp = _peer(h_p, w, tt_p, et)
        xs = _peer(h_s, w, tt_s, et)
        outs[3].append(c.reshape(db, n_q, kv_lora))
        outs[4].append(krp[:, rope_sl].reshape(db, n_q, d_rope))
        outs[5].append(u_full[:, -n_state:])
    return (xp.reshape(batch, seq, d), xs.reshape(db, n_q, d)) + tuple(jnp.stack(o) for o in outs)
```

```python
import functools
import math

import jax
import jax.numpy as jnp
from jax import lax
from jax.experimental import pallas as pl
from jax.experimental.pallas import tpu as pltpu

F32 = jnp.float32
BF16 = jnp.bfloat16

EPS = 1e-6
ROPE_THETA = 10000.0
POOL_WINDOWS = (2, 4, 8, 16)
PEER_TOPK = 16
NEG = -1e30
LANES = 128
HEAD_PAD = 128
VMEM_LIMIT = 56 * 1024 * 1024


def _nt(a, b):
    return lax.dot_general(a, b, (((1,), (1,)), ((), ())), preferred_element_type=F32)


def _mm(a, b):
    return jnp.dot(a, b, preferred_element_type=F32)


def _rms(x, w):
    return x * lax.rsqrt(jnp.mean(x * x, axis=-1, keepdims=True) + EPS) * w


def _params(**kw):
    return pltpu.CompilerParams(vmem_limit_bytes=VMEM_LIMIT, **kw)


def _full(shape):
    n = len(shape)
    return pl.BlockSpec(shape, lambda *_: (0,) * n)


def _head_norm(blk, ta, tb, w, d_head):
    rot = pltpu.roll(blk, HEAD_PAD - 32, axis=1)
    hq = blk * ta + rot * tb
    ssq = jnp.sum(hq * hq, axis=-1, keepdims=True)
    return hq * lax.rsqrt(ssq * (1.0 / d_head) + EPS) * w


def _proj_body(x_ref, ta_ref, tb_ref, n1_ref, win_ref, qlw_ref, wq_ref, kvw_ref, wuk_ref, wuv_ref,
               qnw_ref, knw_ref, q_ref, k_ref, v_ref, c_ref, kr_ref, u_ref, g_ref,
               *, q_lora, kv_lora, pool_dim, n_heads, d_head, d_v):
    x = x_ref[...]
    xn = _rms(x, n1_ref[...]).astype(BF16)
    proj = _mm(xn, win_ref[...])
    o1 = q_lora
    o2 = o1 + kv_lora
    o3 = o2 + HEAD_PAD
    o4 = o3 + pool_dim
    ta = ta_ref[...]
    tb = tb_ref[...]

    ql = _rms(proj[:, :o1], qlw_ref[...]).astype(BF16)
    qu = _mm(ql, wq_ref[...])
    c = _rms(proj[:, o1:o2], kvw_ref[...])
    c_ref[...] = c
    cb = c.astype(BF16)
    krb = proj[:, o2:o3]
    krp = krb * ta + pltpu.roll(krb, HEAD_PAD - 32, axis=1) * tb
    kr_ref[...] = krp
    ku = _mm(cb, wuk_ref[...])
    vt = _nt(wuv_ref[...], cb)
    vrow = lax.broadcasted_iota(jnp.int32, vt.shape, 0) % HEAD_PAD
    v_ref[0] = jnp.where(vrow == d_v, 1.0, vt).astype(BF16)
    qnw = qnw_ref[...]
    knw = knw_ref[...]
    for h in range(n_heads):
        sl = slice(h * HEAD_PAD, (h + 1) * HEAD_PAD)
        q_ref[:, sl] = _head_norm(qu[:, sl], ta, tb, qnw, d_head).astype(BF16)
        kh = ku[:, sl] + krp
        ssq = jnp.sum(kh * kh, axis=-1, keepdims=True)
        k_ref[:, sl] = (kh * lax.rsqrt(ssq * (1.0 / d_head) + EPS) * knw).astype(BF16)
    u_ref[...] = proj[:, o3:o4]
    g_ref[...] = jax.nn.sigmoid(proj[:, o4:]).astype(BF16)


def _proj(x2d, ta, tb, w, dims, tm):
    n, d = x2d.shape
    nt = n // tm
    tab_blocks = ta.shape[0] // tm
    row = lambda i: (i, 0)
    tab = lambda i: (i % tab_blocks, 0)
    hp = dims["n_heads"] * HEAD_PAD
    body = functools.partial(_proj_body, q_lora=dims["q_lora"], kv_lora=dims["kv_lora"],
                             pool_dim=dims["pool_dim"], n_heads=dims["n_heads"], d_head=dims["d_head"],
                             d_v=dims["d_v"])
    consts = [w["n1"], w["win"], w["qlw"], w["wq"], w["kvw"], w["wuk"], w["wuvt"], w["qnw"], w["knw"]]
    return pl.pallas_call(
        body,
        grid=(nt,),
        in_specs=[pl.BlockSpec((tm, d), row), pl.BlockSpec((tm, HEAD_PAD), tab), pl.BlockSpec((tm, HEAD_PAD), tab)]
                 + [_full(a.shape) for a in consts],
        out_specs=[pl.BlockSpec((tm, hp), row), pl.BlockSpec((tm, hp), row),
                   pl.BlockSpec((1, hp, tm), lambda i: (i, 0, 0)),
                   pl.BlockSpec((tm, dims["kv_lora"]), row), pl.BlockSpec((tm, HEAD_PAD), row),
                   pl.BlockSpec((tm, dims["pool_dim"]), row), pl.BlockSpec((tm, 2 * d), row)],
        out_shape=[jax.ShapeDtypeStruct((n, hp), BF16), jax.ShapeDtypeStruct((n, hp), BF16),
                   jax.ShapeDtypeStruct((nt, hp, tm), BF16), jax.ShapeDtypeStruct((n, dims["kv_lora"]), F32),
                   jax.ShapeDtypeStruct((n, HEAD_PAD), F32), jax.ShapeDtypeStruct((n, dims["pool_dim"]), F32),
                   jax.ShapeDtypeStruct((n, 2 * d), BF16)],
        compiler_params=_params(dimension_semantics=("parallel",)),
        name="proj",
    )(x2d, ta, tb, *consts)


def _window_sums(ext):
    s2 = ext + pltpu.roll(ext, 1, axis=0)
    s4 = s2 + pltpu.roll(s2, 2, axis=0)
    s8 = s4 + pltpu.roll(s4, 4, axis=0)
    s16 = s8 + pltpu.roll(s8, 8, axis=0)
    return (s2, s4, s8, s16)


def _pool_tail(z_groups, g2, lin_ref, scale_ref, wbr_ref, out_ref):
    zl = [_mm(z.astype(BF16), lin_ref[g]) for g, z in enumerate(z_groups)]
    pool = jnp.concatenate(zl, axis=-1) * scale_ref[...]
    br = _mm(pool.astype(BF16), wbr_ref[...])
    out_ref[...] = (g2.astype(F32) * br).astype(BF16)


def _pool_prompt_body(u_ref, halo_ref, g_ref, lin_ref, scale_ref, wbr_ref, out_ref, *, tm, gd):
    j = pl.program_id(1)
    u = u_ref[...]
    prev = jnp.where(j == 0, 0.0, halo_ref[...])
    sums = _window_sums(jnp.concatenate([prev, u], axis=0))
    pos = j * tm + lax.broadcasted_iota(jnp.int32, (tm, gd), 0)
    zs = []
    for g, wdw in enumerate(POOL_WINDOWS):
        sl = slice(g * gd, (g + 1) * gd)
        cnt = jnp.minimum(pos + 1, wdw).astype(F32)
        zs.append(sums[g][16:, sl] / cnt - u[:, sl])
    _pool_tail(zs, g_ref[...], lin_ref, scale_ref, wbr_ref, out_ref)


def _pool_prompt(u2d, gates, w, batch, seq, tm):
    n, pd = u2d.shape
    d = gates.shape[1] // 2
    ns = seq // tm
    gd = pd // len(POOL_WINDOWS)
    hb = tm // 16
    body = functools.partial(_pool_prompt_body, tm=tm, gd=gd)
    return pl.pallas_call(
        body,
        grid=(batch, ns),
        in_specs=[pl.BlockSpec((tm, pd), lambda b, j: (b * ns + j, 0)),
                  pl.BlockSpec((16, pd), lambda b, j: (jnp.maximum((b * ns + j) * hb - 1, 0), 0)),
                  pl.BlockSpec((tm, d), lambda b, j: (b * ns + j, 1)),
                  _full(w["pool_lin"].shape), _full(w["pool_scale"].shape), _full(w["wpbr"].shape)],
        out_specs=pl.BlockSpec((tm, d), lambda b, j: (b * ns + j, 0)),
        out_shape=jax.ShapeDtypeStruct((n, d), BF16),
        compiler_params=_params(dimension_semantics=("parallel", "parallel")),
        name="pool_prompt",
    )(u2d, u2d, gates, w["pool_lin"], w["pool_scale"], w["wpbr"])


def _pool_sample_body(ext_ref, g_ref, lin_ref, scale_ref, wbr_ref, out_ref, *, nb, rows, n_new, gd):
    ext = ext_ref[...]
    sums = _window_sums(ext)
    pick = lambda a: a.reshape(nb, rows, a.shape[-1])[:, rows - n_new:, :].reshape(nb * n_new, a.shape[-1])
    zs = []
    for g, wdw in enumerate(POOL_WINDOWS):
        sl = slice(g * gd, (g + 1) * gd)
        zs.append(pick(sums[g][:, sl]) / float(wdw) - pick(ext[:, sl]))
    _pool_tail(zs, g_ref[...], lin_ref, scale_ref, wbr_ref, out_ref)


def _pool_sample(ext2d, gates, w, db, rows, n_new):
    pd = ext2d.shape[1]
    d = gates.shape[1] // 2
    gd = pd // len(POOL_WINDOWS)
    nb = _tile(db, 16)
    body = functools.partial(_pool_sample_body, nb=nb, rows=rows, n_new=n_new, gd=gd)
    return pl.pallas_call(
        body,
        grid=(db // nb,),
        in_specs=[pl.BlockSpec((nb * rows, pd), lambda i: (i, 0)), pl.BlockSpec((nb * n_new, d), lambda i: (i, 1)),
                  _full(w["pool_lin"].shape), _full(w["pool_scale"].shape), _full(w["wpbr"].shape)],
        out_specs=pl.BlockSpec((nb * n_new, d), lambda i: (i, 0)),
        out_shape=jax.ShapeDtypeStruct((db * n_new, d), BF16),
        compiler_params=_params(dimension_semantics=("parallel",)),
        name="pool_sample",
    )(ext2d, gates, w["pool_lin"], w["pool_scale"], w["wpbr"])


def _flash_body(q_ref, k_ref, vt_ref, o_ref, m_sc, acc_sc, *, tq, hps, l_row):
    i = pl.program_id(2)
    tv = vt_ref.shape[2]
    m_sc[...] = jnp.full(m_sc.shape, NEG, F32)
    acc_sc[...] = jnp.zeros(acc_sc.shape, F32)

    def chunk(j, masked):
        r0 = pl.multiple_of(j * tq, tq)
        for hh in range(hps):
            sl = slice(hh * HEAD_PAD, (hh + 1) * HEAD_PAD)
            st = _nt(k_ref[pl.ds(r0, tq), sl], q_ref[:, sl])
            if masked:
                kpos = lax.broadcasted_iota(jnp.int32, st.shape, 0)
                qpos = lax.broadcasted_iota(jnp.int32, st.shape, 1)
                st = jnp.where(kpos <= qpos, st, NEG)
            m_old = m_sc[hh]
            m_new = jnp.maximum(m_old, jnp.max(st, axis=0, keepdims=True))
            p = jnp.exp(st - m_new).astype(BF16)
            pv = _mm(vt_ref[j * (tq // tv), sl, :], p[0:tv])
            for c in range(1, tq // tv):
                pv = pv + _mm(vt_ref[j * (tq // tv) + c, sl, :], p[c * tv:(c + 1) * tv])
            acc_sc[hh] = acc_sc[hh] * jnp.exp(m_old - m_new) + pv
            m_sc[hh] = m_new

    def full_chunk(j, carry):
        chunk(j, False)
        return carry

    lax.fori_loop(0, i, full_chunk, 0)
    chunk(i, True)
    for hh in range(hps):
        acc = acc_sc[hh]
        o_ref[:, hh * HEAD_PAD:(hh + 1) * HEAD_PAD] = (acc / acc[l_row:l_row + 1, :]).T.astype(BF16)


def _flash(q, k, vt, batch, seq, n_heads, tq, hps, l_row):
    n = q.shape[0]
    nq = seq // tq
    w = hps * HEAD_PAD
    tv = vt.shape[2]
    body = functools.partial(_flash_body, tq=tq, hps=hps, l_row=l_row)
    return pl.pallas_call(
        body,
        grid=(batch, n_heads // hps, nq),
        in_specs=[pl.BlockSpec((tq, w), lambda b, h, i: (b * nq + i, h)),
                  pl.BlockSpec((seq, w), lambda b, h, i: (b, h)),
                  pl.BlockSpec((seq // tv, w, tv), lambda b, h, i: (b, h, 0))],
        out_specs=pl.BlockSpec((tq, w), lambda b, h, i: (b * nq + i, h)),
        out_shape=jax.ShapeDtypeStruct((n, n_heads * HEAD_PAD), BF16),
        scratch_shapes=[pltpu.VMEM((hps, 1, tq), F32), pltpu.VMEM((hps, HEAD_PAD, tq), F32)],
        compiler_params=_params(dimension_semantics=("parallel", "parallel", "arbitrary")),
        name="flash",
    )(q, k, vt)


def _sattn_body(pt_ref, q_ref, cn_ref, krn_ref, wukt_ref, wukh_ref, knw_ref, wuv_ref, *rest,
                n_pg, pg_chunk, n_heads, n_q, d_nope, d_rope, d_head):
    c_refs = rest[:n_pg]
    kr_refs = rest[n_pg:2 * n_pg]
    o_ref = rest[2 * n_pg]
    wall_sc, qp_sc, m_sc, l_sc, acc_sc = rest[2 * n_pg + 1:]
    s_idx = pl.program_id(1)
    n_kn = n_heads * d_nope
    rows = n_heads * n_q

    @pl.when(s_idx == 0)
    def _():
        knw = knw_ref[...]
        qcs = []
        for h in range(n_heads):
            qp = q_ref[:, h * HEAD_PAD:(h + 1) * HEAD_PAD] * knw
            qp_sc[h * n_q:(h + 1) * n_q, :] = qp
            qcs.append(_nt(qp.astype(BF16), wukh_ref[h]))
        wall_sc[0:n_kn, :] = wukt_ref[...]
        wall_sc[n_kn:n_kn + rows, :] = jnp.concatenate(qcs, axis=0).astype(BF16)
        m_sc[...] = jnp.full(m_sc.shape, NEG, F32)
        l_sc[...] = jnp.zeros(l_sc.shape, F32)
        acc_sc[...] = jnp.zeros(acc_sc.shape, F32)

    def scores(cb, s_rope, krs):
        big = _nt(wall_sc[...], cb)
        knt = big[0:n_kn]
        sq = knt * knt
        ssq = jnp.sum(sq.reshape(n_heads, d_nope, sq.shape[-1]), axis=1) + krs
        r = lax.rsqrt(ssq * (1.0 / d_head) + EPS)
        s_raw = big[n_kn:n_kn + rows] + s_rope
        return jnp.concatenate([s_raw[h * n_q:(h + 1) * n_q, :] * r[h:h + 1, :] for h in range(n_heads)], axis=0)

    def update(s, cb):
        m_old = m_sc[...]
        m_new = jnp.maximum(m_old, jnp.max(s, axis=-1, keepdims=True))
        corr = jnp.exp(m_old - m_new)
        p = jnp.exp(s - m_new)
        l_sc[...] = l_sc[...] * corr + jnp.sum(p, axis=-1, keepdims=True)
        acc_sc[...] = acc_sc[...] * corr + _mm(p.astype(BF16), cb)
        m_sc[...] = m_new

    qr = qp_sc[:, d_nope:d_nope + d_rope].astype(BF16)
    n_chunks = n_pg // pg_chunk
    halves = 2 if n_chunks % 2 == 0 else 1
    per_half = n_chunks // halves
    for hf in range(halves):
        ss, cbs = [], []
        for ck in range(hf * per_half, (hf + 1) * per_half):
            pgs = range(ck * pg_chunk, (ck + 1) * pg_chunk)
            cb = jnp.concatenate([c_refs[p][0, 0] for p in pgs], axis=0).astype(BF16)
            krt = jnp.concatenate([kr_refs[p][0, 0] for p in pgs], axis=1)
            krs = jnp.sum(krt * krt, axis=0, keepdims=True)
            ss.append(scores(cb, _mm(qr, krt.astype(BF16)), krs))
            cbs.append(cb)
        update(jnp.concatenate(ss, axis=1), jnp.concatenate(cbs, axis=0))

    @pl.when(s_idx == pl.num_programs(1) - 1)
    def _():
        cn = cn_ref[0].astype(BF16)
        krn = krn_ref[0]
        kk = krn * krn
        hi = kk.astype(BF16)
        lo = (kk - hi.astype(F32)).astype(BF16)
        ones = jnp.ones((n_heads, kk.shape[1]), BF16)
        krs = _nt(ones, hi) + _nt(ones, lo)
        qpos = lax.broadcasted_iota(jnp.int32, (rows, cn.shape[0]), 0) % n_q
        tpos = lax.broadcasted_iota(jnp.int32, (rows, cn.shape[0]), 1)
        s = scores(cn, _nt(qp_sc[...].astype(BF16), krn.astype(BF16)), krs)
        update(jnp.where(tpos <= qpos, s, NEG), cn)
        lat = (acc_sc[...] / l_sc[...]).astype(BF16)
        for h in range(n_heads):
            sl = slice(h * HEAD_PAD, (h + 1) * HEAD_PAD)
            o_ref[:, sl] = _mm(lat[h * n_q:(h + 1) * n_q, :], wuv_ref[:, sl])


def _sattn(page_table, q, c_new_pad, kr_new_pad, ckv, krope_t, layer, w, dims, n_pg):
    db, n_pages = page_table.shape
    n_q = q.shape[0] // db
    page, kv_lora = ckv.shape[2], ckv.shape[3]
    d_rope = krope_t.shape[2]
    n_heads = dims["n_heads"]
    hp = n_heads * HEAD_PAD
    steps = n_pages // n_pg
    rows = n_heads * n_q
    n_kn = n_heads * dims["d_nope"]
    pg_chunk = 2 if n_pg % 2 == 0 else 1
    body = functools.partial(_sattn_body, n_pg=n_pg, pg_chunk=pg_chunk, n_heads=n_heads, n_q=n_q,
                             d_nope=dims["d_nope"], d_rope=d_rope, d_head=dims["d_head"])
    pt = page_table.reshape(-1)

    def page_map(p):
        return lambda b, s, pt_ref: (layer, pt_ref[b * n_pages + s * n_pg + p], 0, 0)

    fixed = lambda shape: pl.BlockSpec(shape, lambda b, s, pt_ref: (0,) * len(shape))
    grid_spec = pltpu.PrefetchScalarGridSpec(
        num_scalar_prefetch=1,
        grid=(db, steps),
        in_specs=[pl.BlockSpec((n_q, hp), lambda b, s, pt_ref: (b, 0)),
                  pl.BlockSpec((1, page, kv_lora), lambda b, s, pt_ref: (b, 0, 0)),
                  pl.BlockSpec((1, page, HEAD_PAD), lambda b, s, pt_ref: (b, 0, 0)),
                  fixed(w["wukt"].shape), fixed(w["wukh"].shape), fixed(w["knw_raw"].shape), fixed(w["wuv"].shape)]
                 + [pl.BlockSpec((1, 1, page, kv_lora), page_map(p)) for p in range(n_pg)]
                 + [pl.BlockSpec((1, 1, d_rope, page), page_map(p)) for p in range(n_pg)],
        out_specs=pl.BlockSpec((n_q, hp), lambda b, s, pt_ref: (b, 0)),
        scratch_shapes=[pltpu.VMEM((n_kn + rows, kv_lora), BF16), pltpu.VMEM((rows, HEAD_PAD), F32),
                        pltpu.VMEM((rows, 1), F32), pltpu.VMEM((rows, 1), F32), pltpu.VMEM((rows, kv_lora), F32)],
    )
    return pl.pallas_call(
        body,
        grid_spec=grid_spec,
        out_shape=jax.ShapeDtypeStruct((db * n_q, hp), F32),
        compiler_params=_params(dimension_semantics=("parallel", "arbitrary")),
        name="sattn",
    )(pt, q, c_new_pad, kr_new_pad, w["wukt"], w["wukh"], w["knw_raw"], w["wuv"],
      *([ckv] * n_pg), *([krope_t] * n_pg))


def _post_body(x_ref, a_ref, g_ref, pc_ref, wab_ref, wout_ref, h_ref):
    br = _mm(a_ref[...].astype(BF16), wab_ref[...])
    mixed = g_ref[...].astype(F32) * br + pc_ref[...].astype(F32)
    h_ref[...] = x_ref[...] + _mm(mixed.astype(BF16), wout_ref[...])


def _post(x2d, attn, gates, pc, w, tm):
    n, d = x2d.shape
    row = lambda i: (i, 0)
    return pl.pallas_call(
        _post_body,
        grid=(n // tm,),
        in_specs=[pl.BlockSpec((tm, d), row), pl.BlockSpec((tm, attn.shape[1]), row), pl.BlockSpec((tm, d), row),
                  pl.BlockSpec((tm, d), row), _full(w["wab"].shape), _full(w["wout"].shape)],
        out_specs=pl.BlockSpec((tm, d), row),
        out_shape=jax.ShapeDtypeStruct((n, d), F32),
        compiler_params=_params(dimension_semantics=("parallel",)),
        name="post",
    )(x2d, attn, gates, pc, w["wab"], w["wout"])


def _gelu(x):
    k = -2.0 * math.sqrt(2.0 / math.pi) * math.log2(math.e)
    return x / (1.0 + jnp.exp2(x * (k + (k * 0.044715) * (x * x))))


def _batcher_pairs(n):
    pairs = []
    p = 1
    while p < n:
        k = p
        while k >= 1:
            for j in range(k % p, n - k, 2 * k):
                for i in range(min(k, n - j - k)):
                    if (i + j) // (2 * p) == (i + j + k) // (2 * p):
                        pairs.append((i + j, i + j + k))
            k //= 2
        p *= 2
    return pairs


def _top_merge(s, with_rank=False):
    n = PEER_TOPK
    assert s.shape[0] == 8 * n
    st = [s[8 * v:8 * v + 8] for v in range(n)]
    for a, b in _batcher_pairs(n):
        st[a], st[b] = jnp.maximum(st[a], st[b]), jnp.minimum(st[a], st[b])
    for shift in (4, 2, 1):
        other = [pltpu.roll(x, shift, axis=0) for x in st]
        st = [jnp.maximum(st[i], other[n - 1 - i]) for i in range(n)]
        stride = n // 2
        while stride >= 1:
            for i in range(n):
                if (i // stride) % 2 == 0:
                    j = i + stride
                    st[i], st[j] = jnp.maximum(st[i], st[j]), jnp.minimum(st[i], st[j])
            stride //= 2
    rid = lax.broadcasted_iota(jnp.int32, st[0].shape, 0)
    halves = []
    for half in range(n // 8):
        acc = st[half * 8]
        for a in range(1, 8):
            acc = jnp.where(rid == a, st[half * 8 + a], acc)
        halves.append(acc)
    tops = jnp.concatenate(halves, axis=0)
    if not with_rank:
        return tops
    rank = jnp.zeros(s.shape, F32)
    for b in range(n):
        rank = rank + jnp.where(tops[b:b + 1] > s, 1.0, 0.0)
    return tops, rank


def _top_rows(s, k, with_rank=False):
    r, t = s.shape
    groups = r // 8
    st = [s[8 * v:8 * v + 8] for v in range(groups)]
    for a, b in _batcher_pairs(1 << (groups - 1).bit_length()):
        if b < groups:
            st[a], st[b] = jnp.maximum(st[a], st[b]), jnp.minimum(st[a], st[b])
    rid = lax.broadcasted_iota(jnp.int32, (k, t), 0)
    ninf = jnp.full((1, 8, t), -jnp.inf, F32)

    def step(i, carry):
        stack, tops = carry
        m = jnp.max(stack[0], axis=0, keepdims=True)
        tops = jnp.where(rid == i, m, tops)
        hit = stack[0] == m
        stack = jnp.where(hit[None], jnp.concatenate([stack[1:], ninf], axis=0), stack)
        return stack, tops

    _, tops = lax.fori_loop(0, k, step, (jnp.stack(st), jnp.full((k, t), -jnp.inf, F32)))
    if not with_rank:
        return tops
    rank = jnp.zeros(s.shape, F32)
    for b in range(k):
        rank = rank + jnp.where(tops[b:b + 1] > s, 1.0, 0.0)
    return tops, rank


def _peer_body(h_ref, n2_ref, wqt_ref, keys_ref, u_ref, vt_ref, y_ref,
               hn_sc, qt_sc, r1_sc, e1_sc, b_sc, e0_sc, at_sc, g_sc, acc_sc,
               *, n_heads, n_keys, irows, sub):
    s_idx = pl.program_id(1)
    k = PEER_TOPK
    t = hn_sc.shape[0]

    @pl.when(s_idx == 0)
    def _():
        hn_sc[...] = _rms(h_ref[...], n2_ref[...]).astype(BF16)
        acc_sc[...] = jnp.zeros(acc_sc.shape, F32)
        qrows = qt_sc.shape[0] // 4
        for c in range(4):
            qt_sc[c * qrows:(c + 1) * qrows, :] = _nt(wqt_ref[c * qrows:(c + 1) * qrows, :], hn_sc[...]).astype(BF16)

        def head(h, _):
            def scores(p):
                r0 = pl.multiple_of((h * 2 + p) * n_keys, n_keys)
                return _mm(keys_ref[h * 2 + p], qt_sc[pl.ds(r0, n_keys), :])

            s0 = scores(0)
            s1 = scores(1)
            top0 = _top_merge(s0)
            top1, rank1 = _top_merge(s1, with_rank=True)
            cand = jnp.concatenate(
                [top0[0:1] + top1]
                + [top0[a:a + 1] + top1[0:8] for a in range(1, 8)]
                + [top0[8:16] + top1[0:1]], axis=0)
            ctop = _top_rows(cand, k + 1)
            c_k = ctop[k - 1:k]
            tau = c_k + 0.5 * (ctop[k:k + 1] - c_k)
            m = top0[0:1] + top1[0:1]
            z = jnp.sum(jnp.where(cand >= c_k, jnp.exp(cand - m), 0.0), axis=0, keepdims=True)
            in0 = s0 >= top0[k - 1:k]
            cnt = jnp.zeros(s0.shape, F32)
            for b in range(k):
                cnt = cnt + jnp.where(s0 + top1[b:b + 1] >= tau, 1.0, 0.0)
            b_sc[h] = jnp.where(in0, cnt, 0.0)
            e0_sc[h] = jnp.where(in0, jnp.exp(s0 - top0[0:1]) / z, 0.0)
            e1_sc[h] = jnp.exp(s1 - top1[0:1]).astype(BF16)
            r1_sc[h] = rank1.astype(BF16)
            return 0

        lax.fori_loop(0, n_heads, head, 0)

    at_sc[...] = _gelu(_nt(u_ref[...], hn_sc[...])).astype(BF16)
    base = pl.multiple_of(s_idx * irows, irows)
    for sb in range(irows // sub):
        for lt in range(t // LANES):
            ls = slice(lt * LANES, (lt + 1) * LANES)
            b_t = [b_sc[h, pl.ds(base, irows), ls] for h in range(n_heads)]
            e0_t = [e0_sc[h, pl.ds(base, irows), ls] for h in range(n_heads)]
            for ii in range(sb * sub, (sb + 1) * sub):
                rs = slice(ii * n_keys, (ii + 1) * n_keys)
                wgt = jnp.zeros((n_keys, LANES), BF16)
                for h in range(n_heads):
                    sel = r1_sc[h, :, ls] < b_t[h][ii:ii + 1, :].astype(BF16)
                    wgt = wgt + jnp.where(sel, e1_sc[h, :, ls] * e0_t[h][ii:ii + 1, :].astype(BF16),
                                          jnp.zeros((), BF16))
                g_sc[rs, ls] = wgt * at_sc[rs, ls]
        cs = slice(sb * sub * n_keys, (sb + 1) * sub * n_keys)
        acc_sc[...] += _mm(vt_ref[0, :, cs], g_sc[cs, :])

    @pl.when(s_idx == pl.num_programs(1) - 1)
    def _():
        y_ref[...] = h_ref[...] + acc_sc[...].T


def _peer(h2d, w, tt, et):
    n, d = h2d.shape
    n_heads, n_keys = w["n_peer_heads"], w["n_keys"]
    n_exp = w["pu"].shape[0]
    irows = et // n_keys
    n_blk = n_exp // et
    body = functools.partial(_peer_body, n_heads=n_heads, n_keys=n_keys, irows=irows, sub=min(irows, 2))
    small32 = pltpu.VMEM((n_heads, n_keys, tt), F32)
    small16 = pltpu.VMEM((n_heads, n_keys, tt), BF16)
    return pl.pallas_call(
        body,
        grid=(n // tt, n_blk),
        in_specs=[pl.BlockSpec((tt, d), lambda i, s: (i, 0)), _full(w["n2"].shape), _full(w["wqt"].shape),
                  _full(w["keys"].shape),
                  pl.BlockSpec((et, d), lambda i, s: (s, 0)),
                  pl.BlockSpec((1, d, et), lambda i, s: (s, 0, 0))],
        out_specs=pl.BlockSpec((tt, d), lambda i, s: (i, 0)),
        out_shape=jax.ShapeDtypeStruct((n, d), F32),
        scratch_shapes=[pltpu.VMEM((tt, d), BF16), pltpu.VMEM((w["wqt"].shape[0], tt), BF16),
                        small16, small16, small32, small32,
                        pltpu.VMEM((et, tt), BF16), pltpu.VMEM((et, tt), BF16), pltpu.VMEM((d, tt), F32)],
        compiler_params=_params(dimension_semantics=("parallel", "arbitrary")),
        name="peer",
    )(h2d, w["n2"], w["wqt"], w["keys"], w["pu"], w["pvt"])


def _rot_cols(wr):
    half = wr.shape[-1] // 2
    return jnp.concatenate([-wr[..., half:], wr[..., :half]], axis=-1)


def _rope_tables(pos, d_nope, d_rope):
    half = d_rope // 2
    inv_freq = ROPE_THETA ** (-jnp.arange(half, dtype=F32) / half)
    ang = pos.astype(F32)[:, None] * inv_freq[None, :]
    cos = jnp.cos(ang)
    sin = jnp.sin(ang)
    n = pos.shape[0]
    pad = jnp.zeros((n, HEAD_PAD - d_nope - d_rope), F32)
    ta = jnp.concatenate([jnp.ones((n, d_nope), F32), cos, cos, pad], axis=-1)
    tb = jnp.concatenate([jnp.zeros((n, d_nope), F32), sin, sin, pad], axis=-1)
    return ta, tb


def _layer_weights(l, norm1_w, w_in, q_lat_norm_w, w_q_up, kv_lat_norm_w, w_uk, w_uv, q_norm_w, k_norm_w,
                   w_attn_br, pool_lin_w, pool_scale, w_pool_br, w_out, norm2_w, peer_w_q, peer_keys,
                   peer_u, peer_v, d_rope):
    d = w_in.shape[1]
    q_lora = w_q_up.shape[1]
    n_heads, d_head = w_q_up.shape[2], w_q_up.shape[3]
    kv_lora = w_uk.shape[1]
    d_nope = w_uk.shape[3]
    d_v = w_uv.shape[3]
    off_kv = q_lora
    off_kr = off_kv + kv_lora
    off_pool = off_kr + d_rope
    scale = d_head ** -0.5
    wi = w_in[l]
    w_kr = wi[:, off_kr:off_pool]
    krblk = jnp.concatenate([jnp.zeros((d, d_nope), F32), w_kr, _rot_cols(w_kr)], axis=-1)
    win = jnp.concatenate([wi[:, :off_kr], krblk, wi[:, off_pool:]], axis=-1).astype(BF16)
    wq = w_q_up[l]
    wq = jnp.concatenate([wq[..., :d_nope], wq[..., d_nope:], _rot_cols(wq[..., d_nope:])], axis=-1)
    wq = wq.reshape(q_lora, n_heads * HEAD_PAD).astype(BF16)
    padk = lambda a, width: jnp.concatenate([a, jnp.zeros(a.shape[:-1] + (HEAD_PAD - width,), F32)], axis=-1)
    wuk_pad = padk(w_uk[l], d_nope)
    wuv_pad = padk(w_uv[l], d_v)
    wab = jnp.concatenate([w_attn_br[l].reshape(n_heads, d_v, d),
                           jnp.zeros((n_heads, HEAD_PAD - d_v, d), F32)], axis=1).reshape(n_heads * HEAD_PAD, d)
    n_ph, _, n_keys, half = peer_keys.shape[1:]
    return dict(
        n1=norm1_w[l][None, :], win=win, qlw=q_lat_norm_w[l][None, :], wq=wq, kvw=kv_lat_norm_w[l][None, :],
        wuk=wuk_pad.reshape(kv_lora, n_heads * HEAD_PAD).astype(BF16),
        wuv=wuv_pad.reshape(kv_lora, n_heads * HEAD_PAD).astype(BF16),
        wuvt=wuv_pad.reshape(kv_lora, n_heads * HEAD_PAD).T.astype(BF16),
        qnw=padk(q_norm_w[l], d_head)[None, :] * scale, knw=padk(k_norm_w[l], d_head)[None, :],
        knw_raw=padk(k_norm_w[l], d_head)[None, :],
        wukt=jnp.transpose(w_uk[l], (1, 2, 0)).reshape(n_heads * d_nope, kv_lora).astype(BF16),
        wukh=jnp.transpose(wuk_pad, (1, 0, 2)).astype(BF16),
        wab=wab.astype(BF16), wout=w_out[l].astype(BF16),
        pool_lin=pool_lin_w[l].astype(BF16), pool_scale=pool_scale[l][None, :], wpbr=w_pool_br[l].astype(BF16),
        n2=norm2_w[l][None, :], wqt=peer_w_q[l].T.astype(BF16),
        keys=peer_keys[l].reshape(n_ph * 2, n_keys, half).astype(BF16),
        pu=peer_u[l].astype(BF16),
        n_peer_heads=n_ph, n_keys=n_keys,
    ), dict(q_lora=q_lora, kv_lora=kv_lora, pool_dim=off_pool_dim(wi, off_pool, d), n_heads=n_heads,
            d_head=d_head, d_nope=d_nope, d_v=d_v)


def off_pool_dim(wi, off_pool, d):
    return wi.shape[1] - off_pool - 2 * d


def _tile(n, pref):
    t = pref
    while n % t:
        t //= 2
    return t


def kernel(x_prompt, x_sample, cache_ckv, cache_krope, state_pool, page_table, norm1_w, w_in, q_lat_norm_w, w_q_up, kv_lat_norm_w, w_uk, w_uv, q_norm_w, k_norm_w, w_attn_br, pool_lin_w, pool_scale, w_pool_br, w_out, norm2_w, peer_w_q, peer_keys, peer_u, peer_v):
    batch, seq, d = x_prompt.shape
    db, n_q, _ = x_sample.shape
    depth = w_in.shape[0]
    page = cache_ckv.shape[2]
    d_rope = cache_krope.shape[3]
    n_pages = page_table.shape[1]
    past_len = n_pages * page
    n_state = state_pool.shape[2]
    n_p, n_s = batch * seq, db * n_q
    xp = x_prompt.reshape(n_p, d)
    xs = x_sample.reshape(n_s, d)
    outs = [[] for _ in range(6)]
    for l in range(depth):
        w, dims = _layer_weights(l, norm1_w, w_in, q_lat_norm_w, w_q_up, kv_lat_norm_w, w_uk, w_uv, q_norm_w,
                                 k_norm_w, w_attn_br, pool_lin_w, pool_scale, w_pool_br, w_out, norm2_w,
                                 peer_w_q, peer_keys, peer_u, peer_v, d_rope)
        d_nope = dims["d_nope"]
        kv_lora = dims["kv_lora"]
        pool_dim = dims["pool_dim"]
        rope_sl = slice(d_nope, d_nope + d_rope)
        tm_p = _tile(seq, 256)
        tm_s = _tile(n_s, 256)
        tt_p = _tile(n_p, 512)
        tt_s = _tile(n_s, 512)
        et = _tile(w["pu"].shape[0], 1024)
        w["pvt"] = jnp.swapaxes(peer_v[l].reshape(-1, et, d), 1, 2).astype(BF16)

        ta, tb = _rope_tables(jnp.arange(seq), d_nope, d_rope)
        q, k, v, c, krp, u, gates = _proj(xp, ta, tb, w, dims, tm_p)
        attn = _flash(q, k, v, batch, seq, dims["n_heads"], _tile(seq, 512), _tile(dims["n_heads"], 4),
                      dims["d_v"])
        pc = _pool_prompt(u, gates, w, batch, seq, tm_p)
        h_p = _post(xp, attn, gates, pc, w, tt_p)
        outs[0].append(c.reshape(batch, seq, kv_lora))
        outs[1].append(krp[:, rope_sl].reshape(batch, seq, d_rope))
        outs[2].append(u.reshape(batch, seq, pool_dim)[:, seq - n_state:])

        pos_s = past_len + jnp.arange(n_q)
        ta, tb = _rope_tables(jnp.tile(pos_s, tm_s // n_q), d_nope, d_rope)
        q, _, _, c, krp, u, gates = _proj(xs, ta, tb, w, dims, tm_s)
        zrow = lambda a, rows: jnp.concatenate(
            [a, jnp.zeros((a.shape[0], rows - a.shape[1], a.shape[2]), a.dtype)], axis=1)
        c_new = zrow(c.reshape(db, n_q, kv_lora), page)
        kr_new = zrow(krp.reshape(db, n_q, HEAD_PAD), page)
        n_pg = _tile(n_pages, 32)
        krope_t = jnp.swapaxes(cache_krope, 2, 3)
        attn = _sattn(page_table, q.astype(F32), c_new, kr_new, cache_ckv, krope_t, l, w, dims, n_pg)
        u3 = u.reshape(db, n_q, pool_dim)
        u_full = jnp.concatenate([state_pool[l], u3], axis=1)
        rows = -(-(n_state + n_q + 1) // 8) * 8
        ext = jnp.concatenate([jnp.zeros((db, rows - n_state - n_q, pool_dim), F32), u_full], axis=1)
        pc = _pool_sample(ext.reshape(db * rows, pool_dim), gates, w, db, rows, n_q)
        h_s = _post(xs, attn, gates, pc, w, tt_s)
        xp = _peer(h_p, w, tt_p, et)
        xs = _peer(h_s, w, tt_s, et)
        outs[3].append(c.reshape(db, n_q, kv_lora))
        outs[4].append(krp[:, rope_sl].reshape(db, n_q, d_rope))
        outs[5].append(u_full[:, -n_state:])
    return (xp.reshape(batch, seq, d), xs.reshape(db, n_q, d)) + tuple(jnp.stack(o) for o in outs)
```

```python
import functools
import math

import jax
import jax.numpy as jnp
from jax import lax
from jax.experimental import pallas as pl
from jax.experimental.pallas import tpu as pltpu

F32 = jnp.float32
BF16 = jnp.bfloat16

EPS = 1e-6
ROPE_THETA = 10000.0
POOL_WINDOWS = (2, 4, 8, 16)
PEER_TOPK = 16
NEG = -1e30
LANES = 128
HEAD_PAD = 128
VMEM_LIMIT = 56 * 1024 * 1024


def _nt(a, b):
    return lax.dot_general(a, b, (((1,), (1,)), ((), ())), preferred_element_type=F32)


def _mm(a, b):
    return jnp.dot(a, b, preferred_element_type=F32)


def _rms(x, w):
    return x * lax.rsqrt(jnp.mean(x * x, axis=-1, keepdims=True) + EPS) * w


def _params(**kw):
    return pltpu.CompilerParams(vmem_limit_bytes=VMEM_LIMIT, **kw)


def _full(shape):
    n = len(shape)
    return pl.BlockSpec(shape, lambda *_: (0,) * n)


def _head_norm(blk, ta, tb, w, d_head):
    rot = pltpu.roll(blk, HEAD_PAD - 32, axis=1)
    hq = blk * ta + rot * tb
    ssq = jnp.sum(hq * hq, axis=-1, keepdims=True)
    return hq * lax.rsqrt(ssq * (1.0 / d_head) + EPS) * w


def _proj_body(x_ref, ta_ref, tb_ref, n1_ref, win_ref, qlw_ref, wq_ref, kvw_ref, wuk_ref, wuv_ref,
               qnw_ref, knw_ref, q_ref, k_ref, v_ref, c_ref, kr_ref, u_ref, g_ref,
               *, q_lora, kv_lora, pool_dim, n_heads, d_head, d_v):
    x = x_ref[...]
    xn = _rms(x, n1_ref[...]).astype(BF16)
    proj = _mm(xn, win_ref[...])
    o1 = q_lora
    o2 = o1 + kv_lora
    o3 = o2 + HEAD_PAD
    o4 = o3 + pool_dim
    ta = ta_ref[...]
    tb = tb_ref[...]

    ql = _rms(proj[:, :o1], qlw_ref[...]).astype(BF16)
    qu = _mm(ql, wq_ref[...])
    c = _rms(proj[:, o1:o2], kvw_ref[...])
    c_ref[...] = c
    cb = c.astype(BF16)
    krb = proj[:, o2:o3]
    krp = krb * ta + pltpu.roll(krb, HEAD_PAD - 32, axis=1) * tb
    kr_ref[...] = krp
    ku = _mm(cb, wuk_ref[...])
    vt = _nt(wuv_ref[...], cb)
    vrow = lax.broadcasted_iota(jnp.int32, vt.shape, 0) % HEAD_PAD
    v_ref[0] = jnp.where(vrow == d_v, 1.0, vt).astype(BF16)
    qnw = qnw_ref[...]
    knw = knw_ref[...]
    for h in range(n_heads):
        sl = slice(h * HEAD_PAD, (h + 1) * HEAD_PAD)
        q_ref[:, sl] = _head_norm(qu[:, sl], ta, tb, qnw, d_head).astype(BF16)
        kh = ku[:, sl] + krp
        ssq = jnp.sum(kh * kh, axis=-1, keepdims=True)
        k_ref[:, sl] = (kh * lax.rsqrt(ssq * (1.0 / d_head) + EPS) * knw).astype(BF16)
    u_ref[...] = proj[:, o3:o4]
    g_ref[...] = jax.nn.sigmoid(proj[:, o4:]).astype(BF16)


def _proj(x2d, ta, tb, w, dims, tm):
    n, d = x2d.shape
    nt = n // tm
    tab_blocks = ta.shape[0] // tm
    row = lambda i: (i, 0)
    tab = lambda i: (i % tab_blocks, 0)
    hp = dims["n_heads"] * HEAD_PAD
    body = functools.partial(_proj_body, q_lora=dims["q_lora"], kv_lora=dims["kv_lora"],
                             pool_dim=dims["pool_dim"], n_heads=dims["n_heads"], d_head=dims["d_head"],
                             d_v=dims["d_v"])
    consts = [w["n1"], w["win"], w["qlw"], w["wq"], w["kvw"], w["wuk"], w["wuvt"], w["qnw"], w["knw"]]
    return pl.pallas_call(
        body,
        grid=(nt,),
        in_specs=[pl.BlockSpec((tm, d), row), pl.BlockSpec((tm, HEAD_PAD), tab), pl.BlockSpec((tm, HEAD_PAD), tab)]
                 + [_full(a.shape) for a in consts],
        out_specs=[pl.BlockSpec((tm, hp), row), pl.BlockSpec((tm, hp), row),
                   pl.BlockSpec((1, hp, tm), lambda i: (i, 0, 0)),
                   pl.BlockSpec((tm, dims["kv_lora"]), row), pl.BlockSpec((tm, HEAD_PAD), row),
                   pl.BlockSpec((tm, dims["pool_dim"]), row), pl.BlockSpec((tm, 2 * d), row)],
        out_shape=[jax.ShapeDtypeStruct((n, hp), BF16), jax.ShapeDtypeStruct((n, hp), BF16),
                   jax.ShapeDtypeStruct((nt, hp, tm), BF16), jax.ShapeDtypeStruct((n, dims["kv_lora"]), F32),
                   jax.ShapeDtypeStruct((n, HEAD_PAD), F32), jax.ShapeDtypeStruct((n, dims["pool_dim"]), F32),
                   jax.ShapeDtypeStruct((n, 2 * d), BF16)],
        compiler_params=_params(dimension_semantics=("parallel",)),
        name="proj",
    )(x2d, ta, tb, *consts)


def _window_sums(ext):
    s2 = ext + pltpu.roll(ext, 1, axis=0)
    s4 = s2 + pltpu.roll(s2, 2, axis=0)
    s8 = s4 + pltpu.roll(s4, 4, axis=0)
    s16 = s8 + pltpu.roll(s8, 8, axis=0)
    return (s2, s4, s8, s16)


def _pool_tail(z_groups, g2, lin_ref, scale_ref, wbr_ref, out_ref):
    zl = [_mm(z.astype(BF16), lin_ref[g]) for g, z in enumerate(z_groups)]
    pool = jnp.concatenate(zl, axis=-1) * scale_ref[...]
    br = _mm(pool.astype(BF16), wbr_ref[...])
    out_ref[...] = (g2.astype(F32) * br).astype(BF16)


def _pool_prompt_body(u_ref, halo_ref, g_ref, lin_ref, scale_ref, wbr_ref, out_ref, *, tm, gd):
    j = pl.program_id(1)
    u = u_ref[...]
    prev = jnp.where(j == 0, 0.0, halo_ref[...])
    sums = _window_sums(jnp.concatenate([prev, u], axis=0))
    pos = j * tm + lax.broadcasted_iota(jnp.int32, (tm, gd), 0)
    zs = []
    for g, wdw in enumerate(POOL_WINDOWS):
        sl = slice(g * gd, (g + 1) * gd)
        cnt = jnp.minimum(pos + 1, wdw).astype(F32)
        zs.append(sums[g][16:, sl] / cnt - u[:, sl])
    _pool_tail(zs, g_ref[...], lin_ref, scale_ref, wbr_ref, out_ref)


def _pool_prompt(u2d, gates, w, batch, seq, tm):
    n, pd = u2d.shape
    d = gates.shape[1] // 2
    ns = seq // tm
    gd = pd // len(POOL_WINDOWS)
    hb = tm // 16
    body = functools.partial(_pool_prompt_body, tm=tm, gd=gd)
    return pl.pallas_call(
        body,
        grid=(batch, ns),
        in_specs=[pl.BlockSpec((tm, pd), lambda b, j: (b * ns + j, 0)),
                  pl.BlockSpec((16, pd), lambda b, j: (jnp.maximum((b * ns + j) * hb - 1, 0), 0)),
                  pl.BlockSpec((tm, d), lambda b, j: (b * ns + j, 1)),
                  _full(w["pool_lin"].shape), _full(w["pool_scale"].shape), _full(w["wpbr"].shape)],
        out_specs=pl.BlockSpec((tm, d), lambda b, j: (b * ns + j, 0)),
        out_shape=jax.ShapeDtypeStruct((n, d), BF16),
        compiler_params=_params(dimension_semantics=("parallel", "parallel")),
        name="pool_prompt",
    )(u2d, u2d, gates, w["pool_lin"], w["pool_scale"], w["wpbr"])


def _pool_sample_body(ext_ref, g_ref, lin_ref, scale_ref, wbr_ref, out_ref, *, nb, rows, n_new, gd):
    ext = ext_ref[...]
    sums = _window_sums(ext)
    pick = lambda a: a.reshape(nb, rows, a.shape[-1])[:, rows - n_new:, :].reshape(nb * n_new, a.shape[-1])
    zs = []
    for g, wdw in enumerate(POOL_WINDOWS):
        sl = slice(g * gd, (g + 1) * gd)
        zs.append(pick(sums[g][:, sl]) / float(wdw) - pick(ext[:, sl]))
    _pool_tail(zs, g_ref[...], lin_ref, scale_ref, wbr_ref, out_ref)


def _pool_sample(ext2d, gates, w, db, rows, n_new):
    pd = ext2d.shape[1]
    d = gates.shape[1] // 2
    gd = pd // len(POOL_WINDOWS)
    nb = _tile(db, 16)
    body = functools.partial(_pool_sample_body, nb=nb, rows=rows, n_new=n_new, gd=gd)
    return pl.pallas_call(
        body,
        grid=(db // nb,),
        in_specs=[pl.BlockSpec((nb * rows, pd), lambda i: (i, 0)), pl.BlockSpec((nb * n_new, d), lambda i: (i, 1)),
                  _full(w["pool_lin"].shape), _full(w["pool_scale"].shape), _full(w["wpbr"].shape)],
        out_specs=pl.BlockSpec((nb * n_new, d), lambda i: (i, 0)),
        out_shape=jax.ShapeDtypeStruct((db * n_new, d), BF16),
        compiler_params=_params(dimension_semantics=("parallel",)),
        name="pool_sample",
    )(ext2d, gates, w["pool_lin"], w["pool_scale"], w["wpbr"])


def _flash_body(q_ref, k_ref, vt_ref, o_ref, m_sc, acc_sc, *, tq, hps, l_row):
    i = pl.program_id(2)
    tv = vt_ref.shape[2]
    m_sc[...] = jnp.full(m_sc.shape, NEG, F32)
    acc_sc[...] = jnp.zeros(acc_sc.shape, F32)

    def chunk(j, masked):
        r0 = pl.multiple_of(j * tq, tq)
        for hh in range(hps):
            sl = slice(hh * HEAD_PAD, (hh + 1) * HEAD_PAD)
            st = _nt(k_ref[pl.ds(r0, tq), sl], q_ref[:, sl])
            if masked:
                kpos = lax.broadcasted_iota(jnp.int32, st.shape, 0)
                qpos = lax.broadcasted_iota(jnp.int32, st.shape, 1)
                st = jnp.where(kpos <= qpos, st, NEG)
            m_old = m_sc[hh]
            m_new = jnp.maximum(m_old, jnp.max(st, axis=0, keepdims=True))
            p = jnp.exp(st - m_new).astype(BF16)
            pv = _mm(vt_ref[j * (tq // tv), sl, :], p[0:tv])
            for c in range(1, tq // tv):
                pv = pv + _mm(vt_ref[j * (tq // tv) + c, sl, :], p[c * tv:(c + 1) * tv])
            acc_sc[hh] = acc_sc[hh] * jnp.exp(m_old - m_new) + pv
            m_sc[hh] = m_new

    def full_chunk(j, carry):
        chunk(j, False)
        return carry

    lax.fori_loop(0, i, full_chunk, 0)
    chunk(i, True)
    for hh in range(hps):
        acc = acc_sc[hh]
        o_ref[:, hh * HEAD_PAD:(hh + 1) * HEAD_PAD] = (acc / acc[l_row:l_row + 1, :]).T.astype(BF16)


def _flash(q, k, vt, batch, seq, n_heads, tq, hps, l_row):
    n = q.shape[0]
    nq = seq // tq
    w = hps * HEAD_PAD
    tv = vt.shape[2]
    body = functools.partial(_flash_body, tq=tq, hps=hps, l_row=l_row)
    return pl.pallas_call(
        body,
        grid=(batch, n_heads // hps, nq),
        in_specs=[pl.BlockSpec((tq, w), lambda b, h, i: (b * nq + i, h)),
                  pl.BlockSpec((seq, w), lambda b, h, i: (b, h)),
                  pl.BlockSpec((seq // tv, w, tv), lambda b, h, i: (b, h, 0))],
        out_specs=pl.BlockSpec((tq, w), lambda b, h, i: (b * nq + i, h)),
        out_shape=jax.ShapeDtypeStruct((n, n_heads * HEAD_PAD), BF16),
        scratch_shapes=[pltpu.VMEM((hps, 1, tq), F32), pltpu.VMEM((hps, HEAD_PAD, tq), F32)],
        compiler_params=_params(dimension_semantics=("parallel", "parallel", "arbitrary")),
        name="flash",
    )(q, k, vt)


def _sattn_body(pt_ref, q_ref, cn_ref, krn_ref, wukt_ref, wukh_ref, knw_ref, wuv_ref, *rest,
                n_pg, pg_chunk, n_heads, n_q, d_nope, d_rope, d_head):
    c_refs = rest[:n_pg]
    kr_refs = rest[n_pg:2 * n_pg]
    o_ref = rest[2 * n_pg]
    wall_sc, qp_sc, m_sc, l_sc, acc_sc = rest[2 * n_pg + 1:]
    s_idx = pl.program_id(1)
    n_kn = n_heads * d_nope
    rows = n_heads * n_q

    @pl.when(s_idx == 0)
    def _():
        knw = knw_ref[...]
        qcs = []
        for h in range(n_heads):
            qp = q_ref[:, h * HEAD_PAD:(h + 1) * HEAD_PAD] * knw
            qp_sc[h * n_q:(h + 1) * n_q, :] = qp
            qcs.append(_nt(qp.astype(BF16), wukh_ref[h]))
        wall_sc[0:n_kn, :] = wukt_ref[...]
        wall_sc[n_kn:n_kn + rows, :] = jnp.concatenate(qcs, axis=0).astype(BF16)
        m_sc[...] = jnp.full(m_sc.shape, NEG, F32)
        l_sc[...] = jnp.zeros(l_sc.shape, F32)
        acc_sc[...] = jnp.zeros(acc_sc.shape, F32)

    def scores(cb, s_rope, krs):
        big = _nt(wall_sc[...], cb)
        knt = big[0:n_kn]
        sq = knt * knt
        ssq = jnp.sum(sq.reshape(d_nope, n_heads, sq.shape[-1]), axis=0) + krs
        r = lax.rsqrt(ssq * (1.0 / d_head) + EPS)
        s_raw = big[n_kn:n_kn + rows] + s_rope
        return jnp.concatenate([s_raw[h * n_q:(h + 1) * n_q, :] * r[h:h + 1, :] for h in range(n_heads)], axis=0)

    def update(s, cb):
        m_old = m_sc[...]
        m_new = jnp.maximum(m_old, jnp.max(s, axis=-1, keepdims=True))
        corr = jnp.exp(m_old - m_new)
        p = jnp.exp(s - m_new)
        l_sc[...] = l_sc[...] * corr + jnp.sum(p, axis=-1, keepdims=True)
        acc_sc[...] = acc_sc[...] * corr + _mm(p.astype(BF16), cb)
        m_sc[...] = m_new

    qr = qp_sc[:, d_nope:d_nope + d_rope].astype(BF16)
    n_chunks = n_pg // pg_chunk
    halves = 2 if n_chunks % 2 == 0 else 1
    per_half = n_chunks // halves
    for hf in range(halves):
        ss, cbs = [], []
        for ck in range(hf * per_half, (hf + 1) * per_half):
            pgs = range(ck * pg_chunk, (ck + 1) * pg_chunk)
            cb = jnp.concatenate([c_refs[p][0, 0] for p in pgs], axis=0).astype(BF16)
            krt = jnp.concatenate([kr_refs[p][0, 0] for p in pgs], axis=1)
            krs = jnp.sum(krt * krt, axis=0, keepdims=True)
            ss.append(scores(cb, _mm(qr, krt.astype(BF16)), krs))
            cbs.append(cb)
        update(jnp.concatenate(ss, axis=1), jnp.concatenate(cbs, axis=0))

    @pl.when(s_idx == pl.num_programs(1) - 1)
    def _():
        cn = cn_ref[0].astype(BF16)
        krn = krn_ref[0]
        kk = krn * krn
        hi = kk.astype(BF16)
        lo = (kk - hi.astype(F32)).astype(BF16)
        ones = jnp.ones((n_heads, kk.shape[1]), BF16)
        krs = _nt(ones, hi) + _nt(ones, lo)
        qpos = lax.broadcasted_iota(jnp.int32, (rows, cn.shape[0]), 0) % n_q
        tpos = lax.broadcasted_iota(jnp.int32, (rows, cn.shape[0]), 1)
        s = scores(cn, _nt(qp_sc[...].astype(BF16), krn.astype(BF16)), krs)
        update(jnp.where(tpos <= qpos, s, NEG), cn)
        lat = (acc_sc[...] / l_sc[...]).astype(BF16)
        for h in range(n_heads):
            sl = slice(h * HEAD_PAD, (h + 1) * HEAD_PAD)
            o_ref[:, sl] = _mm(lat[h * n_q:(h + 1) * n_q, :], wuv_ref[:, sl])


def _sattn(page_table, q, c_new_pad, kr_new_pad, ckv, krope_t, layer, w, dims, n_pg):
    db, n_pages = page_table.shape
    n_q = q.shape[0] // db
    page, kv_lora = ckv.shape[2], ckv.shape[3]
    d_rope = krope_t.shape[2]
    n_heads = dims["n_heads"]
    hp = n_heads * HEAD_PAD
    steps = n_pages // n_pg
    rows = n_heads * n_q
    n_kn = n_heads * dims["d_nope"]
    pg_chunk = 2 if n_pg % 2 == 0 else 1
    body = functools.partial(_sattn_body, n_pg=n_pg, pg_chunk=pg_chunk, n_heads=n_heads, n_q=n_q,
                             d_nope=dims["d_nope"], d_rope=d_rope, d_head=dims["d_head"])
    pt = page_table.reshape(-1)

    def page_map(p):
        return lambda b, s, pt_ref: (layer, pt_ref[b * n_pages + s * n_pg + p], 0, 0)

    fixed = lambda shape: pl.BlockSpec(shape, lambda b, s, pt_ref: (0,) * len(shape))
    grid_spec = pltpu.PrefetchScalarGridSpec(
        num_scalar_prefetch=1,
        grid=(db, steps),
        in_specs=[pl.BlockSpec((n_q, hp), lambda b, s, pt_ref: (b, 0)),
                  pl.BlockSpec((1, page, kv_lora), lambda b, s, pt_ref: (b, 0, 0)),
                  pl.BlockSpec((1, page, HEAD_PAD), lambda b, s, pt_ref: (b, 0, 0)),
                  fixed(w["wukt"].shape), fixed(w["wukh"].shape), fixed(w["knw_raw"].shape), fixed(w["wuv"].shape)]
                 + [pl.BlockSpec((1, 1, page, kv_lora), page_map(p)) for p in range(n_pg)]
                 + [pl.BlockSpec((1, 1, d_rope, page), page_map(p)) for p in range(n_pg)],
        out_specs=pl.BlockSpec((n_q, hp), lambda b, s, pt_ref: (b, 0)),
        scratch_shapes=[pltpu.VMEM((n_kn + rows, kv_lora), BF16), pltpu.VMEM((rows, HEAD_PAD), F32),
                        pltpu.VMEM((rows, 1), F32), pltpu.VMEM((rows, 1), F32), pltpu.VMEM((rows, kv_lora), F32)],
    )
    return pl.pallas_call(
        body,
        grid_spec=grid_spec,
        out_shape=jax.ShapeDtypeStruct((db * n_q, hp), F32),
        compiler_params=_params(dimension_semantics=("parallel", "arbitrary")),
        name="sattn",
    )(pt, q, c_new_pad, kr_new_pad, w["wukt"], w["wukh"], w["knw_raw"], w["wuv"],
      *([ckv] * n_pg), *([krope_t] * n_pg))


def _post_body(x_ref, a_ref, g_ref, pc_ref, wab_ref, wout_ref, h_ref):
    br = _mm(a_ref[...].astype(BF16), wab_ref[...])
    mixed = g_ref[...].astype(F32) * br + pc_ref[...].astype(F32)
    h_ref[...] = x_ref[...] + _mm(mixed.astype(BF16), wout_ref[...])


def _post(x2d, attn, gates, pc, w, tm):
    n, d = x2d.shape
    row = lambda i: (i, 0)
    return pl.pallas_call(
        _post_body,
        grid=(n // tm,),
        in_specs=[pl.BlockSpec((tm, d), row), pl.BlockSpec((tm, attn.shape[1]), row), pl.BlockSpec((tm, d), row),
                  pl.BlockSpec((tm, d), row), _full(w["wab"].shape), _full(w["wout"].shape)],
        out_specs=pl.BlockSpec((tm, d), row),
        out_shape=jax.ShapeDtypeStruct((n, d), F32),
        compiler_params=_params(dimension_semantics=("parallel",)),
        name="post",
    )(x2d, attn, gates, pc, w["wab"], w["wout"])


def _gelu(x):
    k = -2.0 * math.sqrt(2.0 / math.pi) * math.log2(math.e)
    return x / (1.0 + jnp.exp2(x * (k + (k * 0.044715) * (x * x))))


def _batcher_pairs(n):
    pairs = []
    p = 1
    while p < n:
        k = p
        while k >= 1:
            for j in range(k % p, n - k, 2 * k):
                for i in range(min(k, n - j - k)):
                    if (i + j) // (2 * p) == (i + j + k) // (2 * p):
                        pairs.append((i + j, i + j + k))
            k //= 2
        p *= 2
    return pairs


def _top_merge(s, with_rank=False):
    n = PEER_TOPK
    assert s.shape[0] == 8 * n
    st = [s[8 * v:8 * v + 8] for v in range(n)]
    for a, b in _batcher_pairs(n):
        st[a], st[b] = jnp.maximum(st[a], st[b]), jnp.minimum(st[a], st[b])
    for shift in (4, 2, 1):
        other = [pltpu.roll(x, shift, axis=0) for x in st]
        st = [jnp.maximum(st[i], other[n - 1 - i]) for i in range(n)]
        stride = n // 2
        while stride >= 1:
            for i in range(n):
                if (i // stride) % 2 == 0:
                    j = i + stride
                    st[i], st[j] = jnp.maximum(st[i], st[j]), jnp.minimum(st[i], st[j])
            stride //= 2
    rid = lax.broadcasted_iota(jnp.int32, st[0].shape, 0)
    halves = []
    for half in range(n // 8):
        acc = st[half * 8]
        for a in range(1, 8):
            acc = jnp.where(rid == a, st[half * 8 + a], acc)
        halves.append(acc)
    tops = jnp.concatenate(halves, axis=0)
    if not with_rank:
        return tops
    rank = jnp.zeros(s.shape, F32)
    for b in range(n):
        rank = rank + jnp.where(tops[b:b + 1] > s, 1.0, 0.0)
    return tops, rank


def _top_rows(s, k, with_rank=False):
    r, t = s.shape
    groups = r // 8
    st = [s[8 * v:8 * v + 8] for v in range(groups)]
    for a, b in _batcher_pairs(1 << (groups - 1).bit_length()):
        if b < groups:
            st[a], st[b] = jnp.maximum(st[a], st[b]), jnp.minimum(st[a], st[b])
    rid = lax.broadcasted_iota(jnp.int32, (k, t), 0)
    ninf = jnp.full((1, 8, t), -jnp.inf, F32)

    def step(i, carry):
        stack, tops = carry
        m = jnp.max(stack[0], axis=0, keepdims=True)
        tops = jnp.where(rid == i, m, tops)
        hit = stack[0] == m
        stack = jnp.where(hit[None], jnp.concatenate([stack[1:], ninf], axis=0), stack)
        return stack, tops

    _, tops = lax.fori_loop(0, k, step, (jnp.stack(st), jnp.full((k, t), -jnp.inf, F32)))
    if not with_rank:
        return tops
    rank = jnp.zeros(s.shape, F32)
    for b in range(k):
        rank = rank + jnp.where(tops[b:b + 1] > s, 1.0, 0.0)
    return tops, rank


def _peer_body(h_ref, n2_ref, wqt_ref, keys_ref, u_ref, vt_ref, y_ref,
               hn_sc, qt_sc, r1_sc, e1_sc, b_sc, e0_sc, g_sc, acc_sc,
               *, n_heads, n_keys, irows):
    s_idx = pl.program_id(1)
    k = PEER_TOPK
    t = hn_sc.shape[0]

    @pl.when(s_idx == 0)
    def _():
        hn_sc[...] = _rms(h_ref[...], n2_ref[...]).astype(BF16)
        acc_sc[...] = jnp.zeros(acc_sc.shape, F32)
        qrows = qt_sc.shape[0] // 4
        for c in range(4):
            qt_sc[c * qrows:(c + 1) * qrows, :] = _nt(wqt_ref[c * qrows:(c + 1) * qrows, :], hn_sc[...]).astype(BF16)

        def head(h, _):
            def scores(p):
                r0 = pl.multiple_of((h * 2 + p) * n_keys, n_keys)
                return _mm(keys_ref[h * 2 + p], qt_sc[pl.ds(r0, n_keys), :])

            s0 = scores(0)
            s1 = scores(1)
            top0 = _top_merge(s0)
            top1, rank1 = _top_merge(s1, with_rank=True)
            cand = jnp.concatenate(
                [top0[0:1] + top1]
                + [top0[a:a + 1] + top1[0:8] for a in range(1, 8)]
                + [top0[8:16] + top1[0:1]], axis=0)
            ctop = _top_rows(cand, k + 1)
            c_k = ctop[k - 1:k]
            tau = c_k + 0.5 * (ctop[k:k + 1] - c_k)
            m = top0[0:1] + top1[0:1]
            z = jnp.sum(jnp.where(cand >= c_k, jnp.exp(cand - m), 0.0), axis=0, keepdims=True)
            in0 = s0 >= top0[k - 1:k]
            cnt = jnp.zeros(s0.shape, F32)
            for b in range(k):
                cnt = cnt + jnp.where(s0 + top1[b:b + 1] >= tau, 1.0, 0.0)
            b_sc[h] = jnp.where(in0, cnt, 0.0)
            e0_sc[h] = jnp.where(in0, jnp.exp(s0 - top0[0:1]) / z, 0.0)
            e1_sc[h] = jnp.exp(s1 - top1[0:1]).astype(BF16)
            r1_sc[h] = rank1.astype(BF16)
            return 0

        lax.fori_loop(0, n_heads, head, 0)

    base = pl.multiple_of(s_idx * irows, irows)
    for lt in range(t // LANES):
        ls = slice(lt * LANES, (lt + 1) * LANES)
        b_t = [b_sc[h, pl.ds(base, irows), ls] for h in range(n_heads)]
        e0_t = [e0_sc[h, pl.ds(base, irows), ls] for h in range(n_heads)]
        for ii in range(irows):
            wgt = jnp.zeros((n_keys, LANES), BF16)
            for h in range(n_heads):
                sel = r1_sc[h, :, ls] < b_t[h][ii:ii + 1, :].astype(BF16)
                wgt = wgt + jnp.where(sel, e1_sc[h, :, ls] * e0_t[h][ii:ii + 1, :].astype(BF16),
                                      jnp.zeros((), BF16))
            g_sc[ii * n_keys:(ii + 1) * n_keys, ls] = wgt
    act = _gelu(_nt(u_ref[...], hn_sc[...])).astype(BF16)
    acc_sc[...] += _mm(vt_ref[0], g_sc[...] * act)

    @pl.when(s_idx == pl.num_programs(1) - 1)
    def _():
        y_ref[...] = h_ref[...] + acc_sc[...].T


def _peer(h2d, w, tt, et):
    n, d = h2d.shape
    n_heads, n_keys = w["n_peer_heads"], w["n_keys"]
    n_exp = w["pu"].shape[0]
    irows = et // n_keys
    n_blk = n_exp // et
    body = functools.partial(_peer_body, n_heads=n_heads, n_keys=n_keys, irows=irows)
    small32 = pltpu.VMEM((n_heads, n_keys, tt), F32)
    small16 = pltpu.VMEM((n_heads, n_keys, tt), BF16)
    return pl.pallas_call(
        body,
        grid=(n // tt, n_blk),
        in_specs=[pl.BlockSpec((tt, d), lambda i, s: (i, 0)), _full(w["n2"].shape), _full(w["wqt"].shape),
                  _full(w["keys"].shape),
                  pl.BlockSpec((et, d), lambda i, s: (s, 0)),
                  pl.BlockSpec((1, d, et), lambda i, s: (s, 0, 0))],
        out_specs=pl.BlockSpec((tt, d), lambda i, s: (i, 0)),
        out_shape=jax.ShapeDtypeStruct((n, d), F32),
        scratch_shapes=[pltpu.VMEM((tt, d), BF16), pltpu.VMEM((w["wqt"].shape[0], tt), BF16),
                        small16, small16, small32, small32,
                        pltpu.VMEM((et, tt), BF16), pltpu.VMEM((d, tt), F32)],
        compiler_params=_params(dimension_semantics=("parallel", "arbitrary")),
        name="peer",
    )(h2d, w["n2"], w["wqt"], w["keys"], w["pu"], w["pvt"])


def _rot_cols(wr):
    half = wr.shape[-1] // 2
    return jnp.concatenate([-wr[..., half:], wr[..., :half]], axis=-1)


def _rope_tables(pos, d_nope, d_rope):
    half = d_rope // 2
    inv_freq = ROPE_THETA ** (-jnp.arange(half, dtype=F32) / half)
    ang = pos.astype(F32)[:, None] * inv_freq[None, :]
    cos = jnp.cos(ang)
    sin = jnp.sin(ang)
    n = pos.shape[0]
    pad = jnp.zeros((n, HEAD_PAD - d_nope - d_rope), F32)
    ta = jnp.concatenate([jnp.ones((n, d_nope), F32), cos, cos, pad], axis=-1)
    tb = jnp.concatenate([jnp.zeros((n, d_nope), F32), sin, sin, pad], axis=-1)
    return ta, tb


def _layer_weights(l, norm1_w, w_in, q_lat_norm_w, w_q_up, kv_lat_norm_w, w_uk, w_uv, q_norm_w, k_norm_w,
                   w_attn_br, pool_lin_w, pool_scale, w_pool_br, w_out, norm2_w, peer_w_q, peer_keys,
                   peer_u, peer_v, d_rope):
    d = w_in.shape[1]
    q_lora = w_q_up.shape[1]
    n_heads, d_head = w_q_up.shape[2], w_q_up.shape[3]
    kv_lora = w_uk.shape[1]
    d_nope = w_uk.shape[3]
    d_v = w_uv.shape[3]
    off_kv = q_lora
    off_kr = off_kv + kv_lora
    off_pool = off_kr + d_rope
    scale = d_head ** -0.5
    wi = w_in[l]
    w_kr = wi[:, off_kr:off_pool]
    krblk = jnp.concatenate([jnp.zeros((d, d_nope), F32), w_kr, _rot_cols(w_kr)], axis=-1)
    win = jnp.concatenate([wi[:, :off_kr], krblk, wi[:, off_pool:]], axis=-1).astype(BF16)
    wq = w_q_up[l]
    wq = jnp.concatenate([wq[..., :d_nope], wq[..., d_nope:], _rot_cols(wq[..., d_nope:])], axis=-1)
    wq = wq.reshape(q_lora, n_heads * HEAD_PAD).astype(BF16)
    padk = lambda a, width: jnp.concatenate([a, jnp.zeros(a.shape[:-1] + (HEAD_PAD - width,), F32)], axis=-1)
    wuk_pad = padk(w_uk[l], d_nope)
    wuv_pad = padk(w_uv[l], d_v)
    wab = jnp.concatenate([w_attn_br[l].reshape(n_heads, d_v, d),
                           jnp.zeros((n_heads, HEAD_PAD - d_v, d), F32)], axis=1).reshape(n_heads * HEAD_PAD, d)
    n_ph, _, n_keys, half = peer_keys.shape[1:]
    return dict(
        n1=norm1_w[l][None, :], win=win, qlw=q_lat_norm_w[l][None, :], wq=wq, kvw=kv_lat_norm_w[l][None, :],
        wuk=wuk_pad.reshape(kv_lora, n_heads * HEAD_PAD).astype(BF16),
        wuv=wuv_pad.reshape(kv_lora, n_heads * HEAD_PAD).astype(BF16),
        wuvt=wuv_pad.reshape(kv_lora, n_heads * HEAD_PAD).T.astype(BF16),
        qnw=padk(q_norm_w[l], d_head)[None, :] * scale, knw=padk(k_norm_w[l], d_head)[None, :],
        knw_raw=padk(k_norm_w[l], d_head)[None, :],
        wukt=jnp.transpose(w_uk[l], (2, 1, 0)).reshape(d_nope * n_heads, kv_lora).astype(BF16),
        wukh=jnp.transpose(wuk_pad, (1, 0, 2)).astype(BF16),
        wab=wab.astype(BF16), wout=w_out[l].astype(BF16),
        pool_lin=pool_lin_w[l].astype(BF16), pool_scale=pool_scale[l][None, :], wpbr=w_pool_br[l].astype(BF16),
        n2=norm2_w[l][None, :], wqt=peer_w_q[l].T.astype(BF16),
        keys=peer_keys[l].reshape(n_ph * 2, n_keys, half).astype(BF16),
        pu=peer_u[l].astype(BF16),
        n_peer_heads=n_ph, n_keys=n_keys,
    ), dict(q_lora=q_lora, kv_lora=kv_lora, pool_dim=off_pool_dim(wi, off_pool, d), n_heads=n_heads,
            d_head=d_head, d_nope=d_nope, d_v=d_v)


def off_pool_dim(wi, off_pool, d):
    return wi.shape[1] - off_pool - 2 * d


def _tile(n, pref):
    t = pref
    while n % t:
        t //= 2
    return t


def kernel(x_prompt, x_sample, cache_ckv, cache_krope, state_pool, page_table, norm1_w, w_in, q_lat_norm_w, w_q_up, kv_lat_norm_w, w_uk, w_uv, q_norm_w, k_norm_w, w_attn_br, pool_lin_w, pool_scale, w_pool_br, w_out, norm2_w, peer_w_q, peer_keys, peer_u, peer_v):
    batch, seq, d = x_prompt.shape
    db, n_q, _ = x_sample.shape
    depth = w_in.shape[0]
    page = cache_ckv.shape[2]
    d_rope = cache_krope.shape[3]
    n_pages = page_table.shape[1]
    past_len = n_pages * page
    n_state = state_pool.shape[2]
    n_p, n_s = batch * seq, db * n_q
    xp = x_prompt.reshape(n_p, d)
    xs = x_sample.reshape(n_s, d)
    outs = [[] for _ in range(6)]
    for l in range(depth):
        w, dims = _layer_weights(l, norm1_w, w_in, q_lat_norm_w, w_q_up, kv_lat_norm_w, w_uk, w_uv, q_norm_w,
                                 k_norm_w, w_attn_br, pool_lin_w, pool_scale, w_pool_br, w_out, norm2_w,
                                 peer_w_q, peer_keys, peer_u, peer_v, d_rope)
        d_nope = dims["d_nope"]
        kv_lora = dims["kv_lora"]
        pool_dim = dims["pool_dim"]
        rope_sl = slice(d_nope, d_nope + d_rope)
        tm_p = _tile(seq, 256)
        tm_s = _tile(n_s, 256)
        tt_p = _tile(n_p, 512)
        tt_s = _tile(n_s, 512)
        et = _tile(w["pu"].shape[0], 1024)
        w["pvt"] = jnp.swapaxes(peer_v[l].reshape(-1, et, d), 1, 2).astype(BF16)

        ta, tb = _rope_tables(jnp.arange(seq), d_nope, d_rope)
        q, k, v, c, krp, u, gates = _proj(xp, ta, tb, w, dims, tm_p)
        attn = _flash(q, k, v, batch, seq, dims["n_heads"], _tile(seq, 512), _tile(dims["n_heads"], 4),
                      dims["d_v"])
        pc = _pool_prompt(u, gates, w, batch, seq, tm_p)
        h_p = _post(xp, attn, gates, pc, w, tt_p)
        outs[0].append(c.reshape(batch, seq, kv_lora))
        outs[1].append(krp[:, rope_sl].reshape(batch, seq, d_rope))
        outs[2].append(u.reshape(batch, seq, pool_dim)[:, seq - n_state:])

        pos_s = past_len + jnp.arange(n_q)
        ta, tb = _rope_tables(jnp.tile(pos_s, tm_s // n_q), d_nope, d_rope)
        q, _, _, c, krp, u, gates = _proj(xs, ta, tb, w, dims, tm_s)
        zrow = lambda a, rows: jnp.concatenate(
            [a, jnp.zeros((a.shape[0], rows - a.shape[1], a.shape[2]), a.dtype)], axis=1)
        c_new = zrow(c.reshape(db, n_q, kv_lora), page)
        kr_new = zrow(krp.reshape(db, n_q, HEAD_PAD), page)
        n_pg = _tile(n_pages, 32)
        krope_t = jnp.swapaxes(cache_krope, 2, 3)
        attn = _sattn(page_table, q.astype(F32), c_new, kr_new, cache_ckv, krope_t, l, w, dims, n_pg)
        u3 = u.reshape(db, n_q, pool_dim)
        u_full = jnp.concatenate([state_pool[l], u3], axis=1)
        rows = -(-(n_state + n_q + 1) // 8) * 8
        ext = jnp.concatenate([jnp.zeros((db, rows - n_state - n_q, pool_dim), F32), u_full], axis=1)
        pc = _pool_sample(ext.reshape(db * rows, pool_dim), gates, w, db, rows, n_q)
        h_s = _post(xs, attn, gates, pc, w, tt_s)
        xp = _peer(h_p, w, tt_p, et)
        xs = _peer(h_s, w, tt_s, et)
        outs[3].append(c.reshape(db, n_q, kv_lora))
        outs[4].append(krp[:, rope_sl].reshape(db, n_q, d_rope))
        outs[5].append(u_full[:, -n_state:])
    return (xp.reshape(batch, seq, d), xs.reshape(db, n_q, d)) + tuple(jnp.stack(o) for o in outs)
```

```python
import functools
import math

import jax
import jax.numpy as jnp
from jax import lax
from jax.experimental import pallas as pl
from jax.experimental.pallas import tpu as pltpu

F32 = jnp.float32
BF16 = jnp.bfloat16

EPS = 1e-6
ROPE_THETA = 10000.0
POOL_WINDOWS = (2, 4, 8, 16)
PEER_TOPK = 16
NEG = -1e30
LANES = 128
HEAD_PAD = 128
VMEM_LIMIT = 56 * 1024 * 1024


def _nt(a, b):
    return lax.dot_general(a, b, (((1,), (1,)), ((), ())), preferred_element_type=F32)


def _mm(a, b):
    return jnp.dot(a, b, preferred_element_type=F32)


def _rms(x, w):
    return x * lax.rsqrt(jnp.mean(x * x, axis=-1, keepdims=True) + EPS) * w


def _params(**kw):
    return pltpu.CompilerParams(vmem_limit_bytes=VMEM_LIMIT, **kw)


def _full(shape):
    n = len(shape)
    return pl.BlockSpec(shape, lambda *_: (0,) * n)


def _head_norm(blk, ta, tb, w, d_head):
    rot = pltpu.roll(blk, HEAD_PAD - 32, axis=1)
    hq = blk * ta + rot * tb
    ssq = jnp.sum(hq * hq, axis=-1, keepdims=True)
    return hq * lax.rsqrt(ssq * (1.0 / d_head) + EPS) * w


def _proj_body(x_ref, ta_ref, tb_ref, n1_ref, win_ref, qlw_ref, wq_ref, kvw_ref, wuk_ref, wuv_ref,
               qnw_ref, knw_ref, q_ref, k_ref, v_ref, c_ref, kr_ref, u_ref, g_ref,
               *, q_lora, kv_lora, pool_dim, n_heads, d_head, d_v):
    x = x_ref[...]
    xn = _rms(x, n1_ref[...]).astype(BF16)
    proj = _mm(xn, win_ref[...])
    o1 = q_lora
    o2 = o1 + kv_lora
    o3 = o2 + HEAD_PAD
    o4 = o3 + pool_dim
    ta = ta_ref[...]
    tb = tb_ref[...]

    ql = _rms(proj[:, :o1], qlw_ref[...]).astype(BF16)
    qu = _mm(ql, wq_ref[...])
    c = _rms(proj[:, o1:o2], kvw_ref[...])
    c_ref[...] = c
    cb = c.astype(BF16)
    krb = proj[:, o2:o3]
    krp = krb * ta + pltpu.roll(krb, HEAD_PAD - 32, axis=1) * tb
    kr_ref[...] = krp
    ku = _mm(cb, wuk_ref[...])
    vt = _nt(wuv_ref[...], cb)
    vrow = lax.broadcasted_iota(jnp.int32, vt.shape, 0) % HEAD_PAD
    v_ref[0] = jnp.where(vrow == d_v, 1.0, vt).astype(BF16)
    qnw = qnw_ref[...]
    knw = knw_ref[...]
    for h in range(n_heads):
        sl = slice(h * HEAD_PAD, (h + 1) * HEAD_PAD)
        q_ref[:, sl] = _head_norm(qu[:, sl], ta, tb, qnw, d_head).astype(BF16)
        kh = ku[:, sl] + krp
        ssq = jnp.sum(kh * kh, axis=-1, keepdims=True)
        k_ref[:, sl] = (kh * lax.rsqrt(ssq * (1.0 / d_head) + EPS) * knw).astype(BF16)
    u_ref[...] = proj[:, o3:o4]
    g_ref[...] = jax.nn.sigmoid(proj[:, o4:]).astype(BF16)


def _proj(x2d, ta, tb, w, dims, tm):
    n, d = x2d.shape
    nt = n // tm
    tab_blocks = ta.shape[0] // tm
    row = lambda i: (i, 0)
    tab = lambda i: (i % tab_blocks, 0)
    hp = dims["n_heads"] * HEAD_PAD
    body = functools.partial(_proj_body, q_lora=dims["q_lora"], kv_lora=dims["kv_lora"],
                             pool_dim=dims["pool_dim"], n_heads=dims["n_heads"], d_head=dims["d_head"],
                             d_v=dims["d_v"])
    consts = [w["n1"], w["win"], w["qlw"], w["wq"], w["kvw"], w["wuk"], w["wuvt"], w["qnw"], w["knw"]]
    return pl.pallas_call(
        body,
        grid=(nt,),
        in_specs=[pl.BlockSpec((tm, d), row), pl.BlockSpec((tm, HEAD_PAD), tab), pl.BlockSpec((tm, HEAD_PAD), tab)]
                 + [_full(a.shape) for a in consts],
        out_specs=[pl.BlockSpec((tm, hp), row), pl.BlockSpec((tm, hp), row),
                   pl.BlockSpec((1, hp, tm), lambda i: (i, 0, 0)),
                   pl.BlockSpec((tm, dims["kv_lora"]), row), pl.BlockSpec((tm, HEAD_PAD), row),
                   pl.BlockSpec((tm, dims["pool_dim"]), row), pl.BlockSpec((tm, 2 * d), row)],
        out_shape=[jax.ShapeDtypeStruct((n, hp), BF16), jax.ShapeDtypeStruct((n, hp), BF16),
                   jax.ShapeDtypeStruct((nt, hp, tm), BF16), jax.ShapeDtypeStruct((n, dims["kv_lora"]), F32),
                   jax.ShapeDtypeStruct((n, HEAD_PAD), F32), jax.ShapeDtypeStruct((n, dims["pool_dim"]), F32),
                   jax.ShapeDtypeStruct((n, 2 * d), BF16)],
        compiler_params=_params(dimension_semantics=("parallel",)),
        name="proj",
    )(x2d, ta, tb, *consts)


def _window_sums(ext):
    s2 = ext + pltpu.roll(ext, 1, axis=0)
    s4 = s2 + pltpu.roll(s2, 2, axis=0)
    s8 = s4 + pltpu.roll(s4, 4, axis=0)
    s16 = s8 + pltpu.roll(s8, 8, axis=0)
    return (s2, s4, s8, s16)


def _pool_tail(z_groups, g2, lin_ref, scale_ref, wbr_ref, out_ref):
    zl = [_mm(z.astype(BF16), lin_ref[g]) for g, z in enumerate(z_groups)]
    pool = jnp.concatenate(zl, axis=-1) * scale_ref[...]
    br = _mm(pool.astype(BF16), wbr_ref[...])
    out_ref[...] = (g2.astype(F32) * br).astype(BF16)


def _pool_prompt_body(u_ref, halo_ref, g_ref, lin_ref, scale_ref, wbr_ref, out_ref, *, tm, gd):
    j = pl.program_id(1)
    u = u_ref[...]
    prev = jnp.where(j == 0, 0.0, halo_ref[...])
    sums = _window_sums(jnp.concatenate([prev, u], axis=0))
    pos = j * tm + lax.broadcasted_iota(jnp.int32, (tm, gd), 0)
    zs = []
    for g, wdw in enumerate(POOL_WINDOWS):
        sl = slice(g * gd, (g + 1) * gd)
        cnt = jnp.minimum(pos + 1, wdw).astype(F32)
        zs.append(sums[g][16:, sl] / cnt - u[:, sl])
    _pool_tail(zs, g_ref[...], lin_ref, scale_ref, wbr_ref, out_ref)


def _pool_prompt(u2d, gates, w, batch, seq, tm):
    n, pd = u2d.shape
    d = gates.shape[1] // 2
    ns = seq // tm
    gd = pd // len(POOL_WINDOWS)
    hb = tm // 16
    body = functools.partial(_pool_prompt_body, tm=tm, gd=gd)
    return pl.pallas_call(
        body,
        grid=(batch, ns),
        in_specs=[pl.BlockSpec((tm, pd), lambda b, j: (b * ns + j, 0)),
                  pl.BlockSpec((16, pd), lambda b, j: (jnp.maximum((b * ns + j) * hb - 1, 0), 0)),
                  pl.BlockSpec((tm, d), lambda b, j: (b * ns + j, 1)),
                  _full(w["pool_lin"].shape), _full(w["pool_scale"].shape), _full(w["wpbr"].shape)],
        out_specs=pl.BlockSpec((tm, d), lambda b, j: (b * ns + j, 0)),
        out_shape=jax.ShapeDtypeStruct((n, d), BF16),
        compiler_params=_params(dimension_semantics=("parallel", "parallel")),
        name="pool_prompt",
    )(u2d, u2d, gates, w["pool_lin"], w["pool_scale"], w["wpbr"])


def _pool_sample_body(ext_ref, g_ref, lin_ref, scale_ref, wbr_ref, out_ref, *, nb, rows, n_new, gd):
    ext = ext_ref[...]
    sums = _window_sums(ext)
    pick = lambda a: a.reshape(nb, rows, a.shape[-1])[:, rows - n_new:, :].reshape(nb * n_new, a.shape[-1])
    zs = []
    for g, wdw in enumerate(POOL_WINDOWS):
        sl = slice(g * gd, (g + 1) * gd)
        zs.append(pick(sums[g][:, sl]) / float(wdw) - pick(ext[:, sl]))
    _pool_tail(zs, g_ref[...], lin_ref, scale_ref, wbr_ref, out_ref)


def _pool_sample(ext2d, gates, w, db, rows, n_new):
    pd = ext2d.shape[1]
    d = gates.shape[1] // 2
    gd = pd // len(POOL_WINDOWS)
    nb = _tile(db, 16)
    body = functools.partial(_pool_sample_body, nb=nb, rows=rows, n_new=n_new, gd=gd)
    return pl.pallas_call(
        body,
        grid=(db // nb,),
        in_specs=[pl.BlockSpec((nb * rows, pd), lambda i: (i, 0)), pl.BlockSpec((nb * n_new, d), lambda i: (i, 1)),
                  _full(w["pool_lin"].shape), _full(w["pool_scale"].shape), _full(w["wpbr"].shape)],
        out_specs=pl.BlockSpec((nb * n_new, d), lambda i: (i, 0)),
        out_shape=jax.ShapeDtypeStruct((db * n_new, d), BF16),
        compiler_params=_params(dimension_semantics=("parallel",)),
        name="pool_sample",
    )(ext2d, gates, w["pool_lin"], w["pool_scale"], w["wpbr"])


def _flash_body(q_ref, k_ref, vt_ref, o_ref, m_sc, acc_sc, *, tq, hps, l_row):
    i = pl.program_id(2)
    tv = vt_ref.shape[2]
    m_sc[...] = jnp.full(m_sc.shape, NEG, F32)
    acc_sc[...] = jnp.zeros(acc_sc.shape, F32)

    def chunk(j, masked):
        r0 = pl.multiple_of(j * tq, tq)
        for hh in range(hps):
            sl = slice(hh * HEAD_PAD, (hh + 1) * HEAD_PAD)
            st = _nt(k_ref[pl.ds(r0, tq), sl], q_ref[:, sl])
            if masked:
                kpos = lax.broadcasted_iota(jnp.int32, st.shape, 0)
                qpos = lax.broadcasted_iota(jnp.int32, st.shape, 1)
                st = jnp.where(kpos <= qpos, st, NEG)
            m_old = m_sc[hh]
            m_new = jnp.maximum(m_old, jnp.max(st, axis=0, keepdims=True))
            p = jnp.exp(st - m_new).astype(BF16)
            pv = _mm(vt_ref[j * (tq // tv), sl, :], p[0:tv])
            for c in range(1, tq // tv):
                pv = pv + _mm(vt_ref[j * (tq // tv) + c, sl, :], p[c * tv:(c + 1) * tv])
            acc_sc[hh] = acc_sc[hh] * jnp.exp(m_old - m_new) + pv
            m_sc[hh] = m_new

    def full_chunk(j, carry):
        chunk(j, False)
        return carry

    lax.fori_loop(0, i, full_chunk, 0)
    chunk(i, True)
    for hh in range(hps):
        acc = acc_sc[hh]
        o_ref[:, hh * HEAD_PAD:(hh + 1) * HEAD_PAD] = (acc / acc[l_row:l_row + 1, :]).T.astype(BF16)


def _flash(q, k, vt, batch, seq, n_heads, tq, hps, l_row):
    n = q.shape[0]
    nq = seq // tq
    w = hps * HEAD_PAD
    tv = vt.shape[2]
    body = functools.partial(_flash_body, tq=tq, hps=hps, l_row=l_row)
    return pl.pallas_call(
        body,
        grid=(batch, n_heads // hps, nq),
        in_specs=[pl.BlockSpec((tq, w), lambda b, h, i: (b * nq + i, h)),
                  pl.BlockSpec((seq, w), lambda b, h, i: (b, h)),
                  pl.BlockSpec((seq // tv, w, tv), lambda b, h, i: (b, h, 0))],
        out_specs=pl.BlockSpec((tq, w), lambda b, h, i: (b * nq + i, h)),
        out_shape=jax.ShapeDtypeStruct((n, n_heads * HEAD_PAD), BF16),
        scratch_shapes=[pltpu.VMEM((hps, 1, tq), F32), pltpu.VMEM((hps, HEAD_PAD, tq), F32)],
        compiler_params=_params(dimension_semantics=("parallel", "parallel", "arbitrary")),
        name="flash",
    )(q, k, vt)


def _sattn_body(pt_ref, q_ref, cn_ref, krn_ref, wukt_ref, wukh_ref, knw_ref, wuv_ref, *rest,
                n_pg, pg_chunk, n_heads, n_q, d_nope, d_rope, d_head):
    c_refs = rest[:n_pg]
    kr_refs = rest[n_pg:2 * n_pg]
    o_ref = rest[2 * n_pg]
    wall_sc, qp_sc, m_sc, l_sc, acc_sc = rest[2 * n_pg + 1:]
    s_idx = pl.program_id(1)
    n_kn = n_heads * d_nope
    rows = n_heads * n_q

    @pl.when(s_idx == 0)
    def _():
        knw = knw_ref[...]
        qcs = []
        for h in range(n_heads):
            qp = q_ref[:, h * HEAD_PAD:(h + 1) * HEAD_PAD] * knw
            qp_sc[h * n_q:(h + 1) * n_q, :] = qp
            qcs.append(_nt(qp.astype(BF16), wukh_ref[h]))
        wall_sc[0:n_kn, :] = wukt_ref[...]
        wall_sc[n_kn:n_kn + rows, :] = jnp.concatenate(qcs, axis=0).astype(BF16)
        m_sc[...] = jnp.full(m_sc.shape, NEG, F32)
        l_sc[...] = jnp.zeros(l_sc.shape, F32)
        acc_sc[...] = jnp.zeros(acc_sc.shape, F32)

    def scores(cb, s_rope, krs):
        big = _nt(wall_sc[...], cb)
        knt = big[0:n_kn]
        sq = knt * knt
        ssq = jnp.sum(sq.reshape(d_nope, n_heads, sq.shape[-1]), axis=0) + krs
        r = lax.rsqrt(ssq * (1.0 / d_head) + EPS)
        s_raw = big[n_kn:n_kn + rows] + s_rope
        return jnp.concatenate([s_raw[h * n_q:(h + 1) * n_q, :] * r[h:h + 1, :] for h in range(n_heads)], axis=0)

    def update(s, cb):
        m_old = m_sc[...]
        m_new = jnp.maximum(m_old, jnp.max(s, axis=-1, keepdims=True))
        corr = jnp.exp(m_old - m_new)
        p = jnp.exp(s - m_new)
        l_sc[...] = l_sc[...] * corr + jnp.sum(p, axis=-1, keepdims=True)
        acc_sc[...] = acc_sc[...] * corr + _mm(p.astype(BF16), cb)
        m_sc[...] = m_new

    qr = qp_sc[:, d_nope:d_nope + d_rope].astype(BF16)
    n_chunks = n_pg // pg_chunk
    halves = 2 if n_chunks % 2 == 0 else 1
    per_half = n_chunks // halves
    for hf in range(halves):
        ss, cbs = [], []
        for ck in range(hf * per_half, (hf + 1) * per_half):
            pgs = range(ck * pg_chunk, (ck + 1) * pg_chunk)
            cb = jnp.concatenate([c_refs[p][0, 0] for p in pgs], axis=0).astype(BF16)
            krt = jnp.concatenate([kr_refs[p][0, 0] for p in pgs], axis=1)
            krs = jnp.sum(krt * krt, axis=0, keepdims=True)
            ss.append(scores(cb, _mm(qr, krt.astype(BF16)), krs))
            cbs.append(cb)
        update(jnp.concatenate(ss, axis=1), jnp.concatenate(cbs, axis=0))

    @pl.when(s_idx == pl.num_programs(1) - 1)
    def _():
        cn = cn_ref[0].astype(BF16)
        krn = krn_ref[0]
        kk = krn * krn
        hi = kk.astype(BF16)
        lo = (kk - hi.astype(F32)).astype(BF16)
        ones = jnp.ones((n_heads, kk.shape[1]), BF16)
        krs = _nt(ones, hi) + _nt(ones, lo)
        qpos = lax.broadcasted_iota(jnp.int32, (rows, cn.shape[0]), 0) % n_q
        tpos = lax.broadcasted_iota(jnp.int32, (rows, cn.shape[0]), 1)
        s = scores(cn, _nt(qp_sc[...].astype(BF16), krn.astype(BF16)), krs)
        update(jnp.where(tpos <= qpos, s, NEG), cn)
        lat = (acc_sc[...] / l_sc[...]).astype(BF16)
        for h in range(n_heads):
            sl = slice(h * HEAD_PAD, (h + 1) * HEAD_PAD)
            o_ref[:, sl] = _mm(lat[h * n_q:(h + 1) * n_q, :], wuv_ref[:, sl])


def _sattn(page_table, q, c_new_pad, kr_new_pad, ckv, krope_t, layer, w, dims, n_pg):
    db, n_pages = page_table.shape
    n_q = q.shape[0] // db
    page, kv_lora = ckv.shape[2], ckv.shape[3]
    d_rope = krope_t.shape[2]
    n_heads = dims["n_heads"]
    hp = n_heads * HEAD_PAD
    steps = n_pages // n_pg
    rows = n_heads * n_q
    n_kn = n_heads * dims["d_nope"]
    pg_chunk = 2 if n_pg % 2 == 0 else 1
    body = functools.partial(_sattn_body, n_pg=n_pg, pg_chunk=pg_chunk, n_heads=n_heads, n_q=n_q,
                             d_nope=dims["d_nope"], d_rope=d_rope, d_head=dims["d_head"])
    pt = page_table.reshape(-1)

    def page_map(p):
        return lambda b, s, pt_ref: (layer, pt_ref[b * n_pages + s * n_pg + p], 0, 0)

    fixed = lambda shape: pl.BlockSpec(shape, lambda b, s, pt_ref: (0,) * len(shape))
    grid_spec = pltpu.PrefetchScalarGridSpec(
        num_scalar_prefetch=1,
        grid=(db, steps),
        in_specs=[pl.BlockSpec((n_q, hp), lambda b, s, pt_ref: (b, 0)),
                  pl.BlockSpec((1, page, kv_lora), lambda b, s, pt_ref: (b, 0, 0)),
                  pl.BlockSpec((1, page, HEAD_PAD), lambda b, s, pt_ref: (b, 0, 0)),
                  fixed(w["wukt"].shape), fixed(w["wukh"].shape), fixed(w["knw_raw"].shape), fixed(w["wuv"].shape)]
                 + [pl.BlockSpec((1, 1, page, kv_lora), page_map(p)) for p in range(n_pg)]
                 + [pl.BlockSpec((1, 1, d_rope, page), page_map(p)) for p in range(n_pg)],
        out_specs=pl.BlockSpec((n_q, hp), lambda b, s, pt_ref: (b, 0)),
        scratch_shapes=[pltpu.VMEM((n_kn + rows, kv_lora), BF16), pltpu.VMEM((rows, HEAD_PAD), F32),
                        pltpu.VMEM((rows, 1), F32), pltpu.VMEM((rows, 1), F32), pltpu.VMEM((rows, kv_lora), F32)],
    )
    return pl.pallas_call(
        body,
        grid_spec=grid_spec,
        out_shape=jax.ShapeDtypeStruct((db * n_q, hp), F32),
        compiler_params=_params(dimension_semantics=("parallel", "arbitrary")),
        name="sattn",
    )(pt, q, c_new_pad, kr_new_pad, w["wukt"], w["wukh"], w["knw_raw"], w["wuv"],
      *([ckv] * n_pg), *([krope_t] * n_pg))


def _post_body(x_ref, a_ref, g_ref, pc_ref, wab_ref, wout_ref, h_ref):
    br = _mm(a_ref[...].astype(BF16), wab_ref[...])
    mixed = g_ref[...].astype(F32) * br + pc_ref[...].astype(F32)
    h_ref[...] = x_ref[...] + _mm(mixed.astype(BF16), wout_ref[...])


def _post(x2d, attn, gates, pc, w, tm):
    n, d = x2d.shape
    row = lambda i: (i, 0)
    return pl.pallas_call(
        _post_body,
        grid=(n // tm,),
        in_specs=[pl.BlockSpec((tm, d), row), pl.BlockSpec((tm, attn.shape[1]), row), pl.BlockSpec((tm, d), row),
                  pl.BlockSpec((tm, d), row), _full(w["wab"].shape), _full(w["wout"].shape)],
        out_specs=pl.BlockSpec((tm, d), row),
        out_shape=jax.ShapeDtypeStruct((n, d), F32),
        compiler_params=_params(dimension_semantics=("parallel",)),
        name="post",
    )(x2d, attn, gates, pc, w["wab"], w["wout"])


def _gelu(x):
    k = -2.0 * math.sqrt(2.0 / math.pi) * math.log2(math.e)
    return x / (1.0 + jnp.exp2(x * (k + (k * 0.044715) * (x * x))))


def _batcher_pairs(n):
    pairs = []
    p = 1
    while p < n:
        k = p
        while k >= 1:
            for j in range(k % p, n - k, 2 * k):
                for i in range(min(k, n - j - k)):
                    if (i + j) // (2 * p) == (i + j + k) // (2 * p):
                        pairs.append((i + j, i + j + k))
            k //= 2
        p *= 2
    return pairs


def _top_merge(s, with_rank=False):
    n = PEER_TOPK
    assert s.shape[0] == 8 * n
    st = [s[8 * v:8 * v + 8] for v in range(n)]
    for a, b in _batcher_pairs(n):
        st[a], st[b] = jnp.maximum(st[a], st[b]), jnp.minimum(st[a], st[b])
    for shift in (4, 2, 1):
        other = [pltpu.roll(x, shift, axis=0) for x in st]
        st = [jnp.maximum(st[i], other[n - 1 - i]) for i in range(n)]
        stride = n // 2
        while stride >= 1:
            for i in range(n):
                if (i // stride) % 2 == 0:
                    j = i + stride
                    st[i], st[j] = jnp.maximum(st[i], st[j]), jnp.minimum(st[i], st[j])
            stride //= 2
    rid = lax.broadcasted_iota(jnp.int32, st[0].shape, 0)
    halves = []
    for half in range(n // 8):
        acc = st[half * 8]
        for a in range(1, 8):
            acc = jnp.where(rid == a, st[half * 8 + a], acc)
        halves.append(acc)
    tops = jnp.concatenate(halves, axis=0)
    if not with_rank:
        return tops
    rank = jnp.zeros(s.shape, F32)
    for b in range(n):
        rank = rank + jnp.where(tops[b:b + 1] > s, 1.0, 0.0)
    return tops, rank


def _top_rows(s, k, with_rank=False):
    r, t = s.shape
    groups = r // 8
    st = [s[8 * v:8 * v + 8] for v in range(groups)]
    for a, b in _batcher_pairs(1 << (groups - 1).bit_length()):
        if b < groups:
            st[a], st[b] = jnp.maximum(st[a], st[b]), jnp.minimum(st[a], st[b])
    rid = lax.broadcasted_iota(jnp.int32, (k, t), 0)
    ninf = jnp.full((1, 8, t), -jnp.inf, F32)

    def step(i, carry):
        stack, tops = carry
        m = jnp.max(stack[0], axis=0, keepdims=True)
        tops = jnp.where(rid == i, m, tops)
        hit = stack[0] == m
        stack = jnp.where(hit[None], jnp.concatenate([stack[1:], ninf], axis=0), stack)
        return stack, tops

    _, tops = lax.fori_loop(0, k, step, (jnp.stack(st), jnp.full((k, t), -jnp.inf, F32)))
    if not with_rank:
        return tops
    rank = jnp.zeros(s.shape, F32)
    for b in range(k):
        rank = rank + jnp.where(tops[b:b + 1] > s, 1.0, 0.0)
    return tops, rank


def _peer_body(h_ref, n2_ref, wqt_ref, keys_ref, u_ref, vt_ref, y_ref,
               hn_sc, qt_sc, r1_sc, e1_sc, b_sc, e0_sc, g_sc, acc_sc,
               *, n_heads, n_keys, irows):
    s_idx = pl.program_id(1)
    k = PEER_TOPK
    t = hn_sc.shape[1]

    @pl.when(s_idx == 0)
    def _():
        hn_sc[...] = _rms(h_ref[...], n2_ref[...]).T.astype(BF16)
        acc_sc[...] = jnp.zeros(acc_sc.shape, F32)
        qrows = qt_sc.shape[0] // 4
        for c in range(4):
            qt_sc[c * qrows:(c + 1) * qrows, :] = _mm(wqt_ref[c * qrows:(c + 1) * qrows, :], hn_sc[...]).astype(BF16)

        def head(h, _):
            def scores(p):
                r0 = pl.multiple_of((h * 2 + p) * n_keys, n_keys)
                return _mm(keys_ref[h * 2 + p], qt_sc[pl.ds(r0, n_keys), :])

            s0 = scores(0)
            s1 = scores(1)
            top0 = _top_merge(s0)
            top1, rank1 = _top_merge(s1, with_rank=True)
            cand = jnp.concatenate(
                [top0[0:1] + top1]
                + [top0[a:a + 1] + top1[0:8] for a in range(1, 8)]
                + [top0[8:16] + top1[0:1]], axis=0)
            ctop = _top_rows(cand, k + 1)
            c_k = ctop[k - 1:k]
            tau = c_k + 0.5 * (ctop[k:k + 1] - c_k)
            m = top0[0:1] + top1[0:1]
            z = jnp.sum(jnp.where(cand >= c_k, jnp.exp(cand - m), 0.0), axis=0, keepdims=True)
            in0 = s0 >= top0[k - 1:k]
            cnt = jnp.zeros(s0.shape, F32)
            for b in range(k):
                cnt = cnt + jnp.where(s0 + top1[b:b + 1] >= tau, 1.0, 0.0)
            b_sc[h] = jnp.where(in0, cnt, 0.0)
            e0_sc[h] = jnp.where(in0, jnp.exp(s0 - top0[0:1]) / z, 0.0)
            e1_sc[h] = jnp.exp(s1 - top1[0:1]).astype(BF16)
            r1_sc[h] = rank1.astype(BF16)
            return 0

        lax.fori_loop(0, n_heads, head, 0)

    base = pl.multiple_of(s_idx * irows, irows)
    for lt in range(t // LANES):
        ls = slice(lt * LANES, (lt + 1) * LANES)
        b_t = [b_sc[h, pl.ds(base, irows), ls] for h in range(n_heads)]
        e0_t = [e0_sc[h, pl.ds(base, irows), ls] for h in range(n_heads)]
        for ii in range(irows):
            wgt = jnp.zeros((n_keys, LANES), BF16)
            for h in range(n_heads):
                sel = r1_sc[h, :, ls] < b_t[h][ii:ii + 1, :].astype(BF16)
                wgt = wgt + jnp.where(sel, e1_sc[h, :, ls] * e0_t[h][ii:ii + 1, :].astype(BF16),
                                      jnp.zeros((), BF16))
            g_sc[ii * n_keys:(ii + 1) * n_keys, ls] = wgt
    act = _gelu(_mm(u_ref[...], hn_sc[...])).astype(BF16)
    acc_sc[...] += _mm(vt_ref[0], g_sc[...] * act)

    @pl.when(s_idx == pl.num_programs(1) - 1)
    def _():
        y_ref[...] = h_ref[...] + acc_sc[...].T


def _peer(h2d, w, tt, et):
    n, d = h2d.shape
    n_heads, n_keys = w["n_peer_heads"], w["n_keys"]
    n_exp = w["pu"].shape[0]
    irows = et // n_keys
    n_blk = n_exp // et
    body = functools.partial(_peer_body, n_heads=n_heads, n_keys=n_keys, irows=irows)
    small32 = pltpu.VMEM((n_heads, n_keys, tt), F32)
    small16 = pltpu.VMEM((n_heads, n_keys, tt), BF16)
    return pl.pallas_call(
        body,
        grid=(n // tt, n_blk),
        in_specs=[pl.BlockSpec((tt, d), lambda i, s: (i, 0)), _full(w["n2"].shape), _full(w["wqt"].shape),
                  _full(w["keys"].shape),
                  pl.BlockSpec((et, d), lambda i, s: (s, 0)),
                  pl.BlockSpec((1, d, et), lambda i, s: (s, 0, 0))],
        out_specs=pl.BlockSpec((tt, d), lambda i, s: (i, 0)),
        out_shape=jax.ShapeDtypeStruct((n, d), F32),
        scratch_shapes=[pltpu.VMEM((d, tt), BF16), pltpu.VMEM((w["wqt"].shape[0], tt), BF16),
                        small16, small16, small32, small32,
                        pltpu.VMEM((et, tt), BF16), pltpu.VMEM((d, tt), F32)],
        compiler_params=_params(dimension_semantics=("parallel", "arbitrary")),
        name="peer",
    )(h2d, w["n2"], w["wqt"], w["keys"], w["pu"], w["pvt"])


def _rot_cols(wr):
    half = wr.shape[-1] // 2
    return jnp.concatenate([-wr[..., half:], wr[..., :half]], axis=-1)


def _rope_tables(pos, d_nope, d_rope):
    half = d_rope // 2
    inv_freq = ROPE_THETA ** (-jnp.arange(half, dtype=F32) / half)
    ang = pos.astype(F32)[:, None] * inv_freq[None, :]
    cos = jnp.cos(ang)
    sin = jnp.sin(ang)
    n = pos.shape[0]
    pad = jnp.zeros((n, HEAD_PAD - d_nope - d_rope), F32)
    ta = jnp.concatenate([jnp.ones((n, d_nope), F32), cos, cos, pad], axis=-1)
    tb = jnp.concatenate([jnp.zeros((n, d_nope), F32), sin, sin, pad], axis=-1)
    return ta, tb


def _layer_weights(l, norm1_w, w_in, q_lat_norm_w, w_q_up, kv_lat_norm_w, w_uk, w_uv, q_norm_w, k_norm_w,
                   w_attn_br, pool_lin_w, pool_scale, w_pool_br, w_out, norm2_w, peer_w_q, peer_keys,
                   peer_u, peer_v, d_rope):
    d = w_in.shape[1]
    q_lora = w_q_up.shape[1]
    n_heads, d_head = w_q_up.shape[2], w_q_up.shape[3]
    kv_lora = w_uk.shape[1]
    d_nope = w_uk.shape[3]
    d_v = w_uv.shape[3]
    off_kv = q_lora
    off_kr = off_kv + kv_lora
    off_pool = off_kr + d_rope
    scale = d_head ** -0.5
    wi = w_in[l]
    w_kr = wi[:, off_kr:off_pool]
    krblk = jnp.concatenate([jnp.zeros((d, d_nope), F32), w_kr, _rot_cols(w_kr)], axis=-1)
    win = jnp.concatenate([wi[:, :off_kr], krblk, wi[:, off_pool:]], axis=-1).astype(BF16)
    wq = w_q_up[l]
    wq = jnp.concatenate([wq[..., :d_nope], wq[..., d_nope:], _rot_cols(wq[..., d_nope:])], axis=-1)
    wq = wq.reshape(q_lora, n_heads * HEAD_PAD).astype(BF16)
    padk = lambda a, width: jnp.concatenate([a, jnp.zeros(a.shape[:-1] + (HEAD_PAD - width,), F32)], axis=-1)
    wuk_pad = padk(w_uk[l], d_nope)
    wuv_pad = padk(w_uv[l], d_v)
    wab = jnp.concatenate([w_attn_br[l].reshape(n_heads, d_v, d),
                           jnp.zeros((n_heads, HEAD_PAD - d_v, d), F32)], axis=1).reshape(n_heads * HEAD_PAD, d)
    n_ph, _, n_keys, half = peer_keys.shape[1:]
    return dict(
        n1=norm1_w[l][None, :], win=win, qlw=q_lat_norm_w[l][None, :], wq=wq, kvw=kv_lat_norm_w[l][None, :],
        wuk=wuk_pad.reshape(kv_lora, n_heads * HEAD_PAD).astype(BF16),
        wuv=wuv_pad.reshape(kv_lora, n_heads * HEAD_PAD).astype(BF16),
        wuvt=wuv_pad.reshape(kv_lora, n_heads * HEAD_PAD).T.astype(BF16),
        qnw=padk(q_norm_w[l], d_head)[None, :] * scale, knw=padk(k_norm_w[l], d_head)[None, :],
        knw_raw=padk(k_norm_w[l], d_head)[None, :],
        wukt=jnp.transpose(w_uk[l], (2, 1, 0)).reshape(d_nope * n_heads, kv_lora).astype(BF16),
        wukh=jnp.transpose(wuk_pad, (1, 0, 2)).astype(BF16),
        wab=wab.astype(BF16), wout=w_out[l].astype(BF16),
        pool_lin=pool_lin_w[l].astype(BF16), pool_scale=pool_scale[l][None, :], wpbr=w_pool_br[l].astype(BF16),
        n2=norm2_w[l][None, :], wqt=peer_w_q[l].T.astype(BF16),
        keys=peer_keys[l].reshape(n_ph * 2, n_keys, half).astype(BF16),
        pu=peer_u[l].astype(BF16),
        n_peer_heads=n_ph, n_keys=n_keys,
    ), dict(q_lora=q_lora, kv_lora=kv_lora, pool_dim=off_pool_dim(wi, off_pool, d), n_heads=n_heads,
            d_head=d_head, d_nope=d_nope, d_v=d_v)


def off_pool_dim(wi, off_pool, d):
    return wi.shape[1] - off_pool - 2 * d


def _tile(n, pref):
    t = pref
    while n % t:
        t //= 2
    return t


def kernel(x_prompt, x_sample, cache_ckv, cache_krope, state_pool, page_table, norm1_w, w_in, q_lat_norm_w, w_q_up, kv_lat_norm_w, w_uk, w_uv, q_norm_w, k_norm_w, w_attn_br, pool_lin_w, pool_scale, w_pool_br, w_out, norm2_w, peer_w_q, peer_keys, peer_u, peer_v):
    batch, seq, d = x_prompt.shape
    db, n_q, _ = x_sample.shape
    depth = w_in.shape[0]
    page = cache_ckv.shape[2]
    d_rope = cache_krope.shape[3]
    n_pages = page_table.shape[1]
    past_len = n_pages * page
    n_state = state_pool.shape[2]
    n_p, n_s = batch * seq, db * n_q
    xp = x_prompt.reshape(n_p, d)
    xs = x_sample.reshape(n_s, d)
    outs = [[] for _ in range(6)]
    for l in range(depth):
        w, dims = _layer_weights(l, norm1_w, w_in, q_lat_norm_w, w_q_up, kv_lat_norm_w, w_uk, w_uv, q_norm_w,
                                 k_norm_w, w_attn_br, pool_lin_w, pool_scale, w_pool_br, w_out, norm2_w,
                                 peer_w_q, peer_keys, peer_u, peer_v, d_rope)
        d_nope = dims["d_nope"]
        kv_lora = dims["kv_lora"]
        pool_dim = dims["pool_dim"]
        rope_sl = slice(d_nope, d_nope + d_rope)
        tm_p = _tile(seq, 256)
        tm_s = _tile(n_s, 256)
        tt_p = _tile(n_p, 512)
        tt_s = _tile(n_s, 512)
        et = _tile(w["pu"].shape[0], 2048)
        w["pvt"] = jnp.swapaxes(peer_v[l].reshape(-1, et, d), 1, 2).astype(BF16)

        ta, tb = _rope_tables(jnp.arange(seq), d_nope, d_rope)
        q, k, v, c, krp, u, gates = _proj(xp, ta, tb, w, dims, tm_p)
        attn = _flash(q, k, v, batch, seq, dims["n_heads"], _tile(seq, 512), _tile(dims["n_heads"], 4),
                      dims["d_v"])
        pc = _pool_prompt(u, gates, w, batch, seq, tm_p)
        h_p = _post(xp, attn, gates, pc, w, tt_p)
        outs[0].append(c.reshape(batch, seq, kv_lora))
        outs[1].append(krp[:, rope_sl].reshape(batch, seq, d_rope))
        outs[2].append(u.reshape(batch, seq, pool_dim)[:, seq - n_state:])

        pos_s = past_len + jnp.arange(n_q)
        ta, tb = _rope_tables(jnp.tile(pos_s, tm_s // n_q), d_nope, d_rope)
        q, _, _, c, krp, u, gates = _proj(xs, ta, tb, w, dims, tm_s)
        zrow = lambda a, rows: jnp.concatenate(
            [a, jnp.zeros((a.shape[0], rows - a.shape[1], a.shape[2]), a.dtype)], axis=1)
        c_new = zrow(c.reshape(db, n_q, kv_lora), page)
        kr_new = zrow(krp.reshape(db, n_q, HEAD_PAD), page)
        n_pg = _tile(n_pages, 32)
        krope_t = jnp.swapaxes(cache_krope, 2, 3)
        attn = _sattn(page_table, q.astype(F32), c_new, kr_new, cache_ckv, krope_t, l, w, dims, n_pg)
        u3 = u.reshape(db, n_q, pool_dim)
        u_full = jnp.concatenate([state_pool[l], u3], axis=1)
        rows = -(-(n_state + n_q + 1) // 8) * 8
        ext = jnp.concatenate([jnp.zeros((db, rows - n_state - n_q, pool_dim), F32), u_full], axis=1)
        pc = _pool_sample(ext.reshape(db * rows, pool_dim), gates, w, db, rows, n_q)
        h_s = _post(xs, attn, gates, pc, w, tt_s)
        xp = _peer(h_p, w, tt_p, et)
        xs = _peer(h_s, w, tt_s, et)
        outs[3].append(c.reshape(db, n_q, kv_lora))
        outs[4].append(krp[:, rope_sl].reshape(db, n_q, d_rope))
        outs[5].append(u_full[:, -n_state:])
    return (xp.reshape(batch, seq, d), xs.reshape(db, n_q, d)) + tuple(jnp.stack(o) for o in outs)
```

```python
import functools
import math

import jax
import jax.numpy as jnp
from jax import lax
from jax.experimental import pallas as pl
from jax.experimental.pallas import tpu as pltpu

F32 = jnp.float32
BF16 = jnp.bfloat16

EPS = 1e-6
ROPE_THETA = 10000.0
POOL_WINDOWS = (2, 4, 8, 16)
PEER_TOPK = 16
NEG = -1e30
LANES = 128
HEAD_PAD = 128
VMEM_LIMIT = 56 * 1024 * 1024


def _nt(a, b):
    return lax.dot_general(a, b, (((1,), (1,)), ((), ())), preferred_element_type=F32)


def _mm(a, b):
    return jnp.dot(a, b, preferred_element_type=F32)


def _rms(x, w):
    return x * lax.rsqrt(jnp.mean(x * x, axis=-1, keepdims=True) + EPS) * w


def _params(**kw):
    return pltpu.CompilerParams(vmem_limit_bytes=VMEM_LIMIT, **kw)


def _full(shape):
    n = len(shape)
    return pl.BlockSpec(shape, lambda *_: (0,) * n)


def _head_norm(blk, ta, tb, w, d_head):
    rot = pltpu.roll(blk, HEAD_PAD - 32, axis=1)
    hq = blk * ta + rot * tb
    ssq = jnp.sum(hq * hq, axis=-1, keepdims=True)
    return hq * lax.rsqrt(ssq * (1.0 / d_head) + EPS) * w


def _proj_body(x_ref, ta_ref, tb_ref, n1_ref, win_ref, qlw_ref, wq_ref, kvw_ref, wuk_ref, wuv_ref,
               qnw_ref, knw_ref, q_ref, k_ref, v_ref, c_ref, kr_ref, u_ref, g_ref,
               *, q_lora, kv_lora, pool_dim, n_heads, d_head, d_v):
    x = x_ref[...]
    xn = _rms(x, n1_ref[...]).astype(BF16)
    proj = _mm(xn, win_ref[...])
    o1 = q_lora
    o2 = o1 + kv_lora
    o3 = o2 + HEAD_PAD
    o4 = o3 + pool_dim
    ta = ta_ref[...]
    tb = tb_ref[...]

    ql = _rms(proj[:, :o1], qlw_ref[...]).astype(BF16)
    qu = _mm(ql, wq_ref[...])
    c = _rms(proj[:, o1:o2], kvw_ref[...])
    c_ref[...] = c
    cb = c.astype(BF16)
    krb = proj[:, o2:o3]
    krp = krb * ta + pltpu.roll(krb, HEAD_PAD - 32, axis=1) * tb
    kr_ref[...] = krp
    ku = _mm(cb, wuk_ref[...])
    vt = _nt(wuv_ref[...], cb)
    vrow = lax.broadcasted_iota(jnp.int32, vt.shape, 0) % HEAD_PAD
    v_ref[0] = jnp.where(vrow == d_v, 1.0, vt).astype(BF16)
    qnw = qnw_ref[...]
    knw = knw_ref[...]
    for h in range(n_heads):
        sl = slice(h * HEAD_PAD, (h + 1) * HEAD_PAD)
        q_ref[:, sl] = _head_norm(qu[:, sl], ta, tb, qnw, d_head).astype(BF16)
        kh = ku[:, sl] + krp
        ssq = jnp.sum(kh * kh, axis=-1, keepdims=True)
        k_ref[:, sl] = (kh * lax.rsqrt(ssq * (1.0 / d_head) + EPS) * knw).astype(BF16)
    u_ref[...] = proj[:, o3:o4]
    g_ref[...] = jax.nn.sigmoid(proj[:, o4:]).astype(BF16)


def _proj(x2d, ta, tb, w, dims, tm):
    n, d = x2d.shape
    nt = n // tm
    tab_blocks = ta.shape[0] // tm
    row = lambda i: (i, 0)
    tab = lambda i: (i % tab_blocks, 0)
    hp = dims["n_heads"] * HEAD_PAD
    body = functools.partial(_proj_body, q_lora=dims["q_lora"], kv_lora=dims["kv_lora"],
                             pool_dim=dims["pool_dim"], n_heads=dims["n_heads"], d_head=dims["d_head"],
                             d_v=dims["d_v"])
    consts = [w["n1"], w["win"], w["qlw"], w["wq"], w["kvw"], w["wuk"], w["wuvt"], w["qnw"], w["knw"]]
    return pl.pallas_call(
        body,
        grid=(nt,),
        in_specs=[pl.BlockSpec((tm, d), row), pl.BlockSpec((tm, HEAD_PAD), tab), pl.BlockSpec((tm, HEAD_PAD), tab)]
                 + [_full(a.shape) for a in consts],
        out_specs=[pl.BlockSpec((tm, hp), row), pl.BlockSpec((tm, hp), row),
                   pl.BlockSpec((1, hp, tm), lambda i: (i, 0, 0)),
                   pl.BlockSpec((tm, dims["kv_lora"]), row), pl.BlockSpec((tm, HEAD_PAD), row),
                   pl.BlockSpec((tm, dims["pool_dim"]), row), pl.BlockSpec((tm, 2 * d), row)],
        out_shape=[jax.ShapeDtypeStruct((n, hp), BF16), jax.ShapeDtypeStruct((n, hp), BF16),
                   jax.ShapeDtypeStruct((nt, hp, tm), BF16), jax.ShapeDtypeStruct((n, dims["kv_lora"]), F32),
                   jax.ShapeDtypeStruct((n, HEAD_PAD), F32), jax.ShapeDtypeStruct((n, dims["pool_dim"]), F32),
                   jax.ShapeDtypeStruct((n, 2 * d), BF16)],
        compiler_params=_params(dimension_semantics=("parallel",)),
        name="proj",
    )(x2d, ta, tb, *consts)


def _window_sums(ext):
    s2 = ext + pltpu.roll(ext, 1, axis=0)
    s4 = s2 + pltpu.roll(s2, 2, axis=0)
    s8 = s4 + pltpu.roll(s4, 4, axis=0)
    s16 = s8 + pltpu.roll(s8, 8, axis=0)
    return (s2, s4, s8, s16)


def _pool_tail(z_groups, g2, lin_ref, scale_ref, wbr_ref, out_ref):
    zl = [_mm(z.astype(BF16), lin_ref[g]) for g, z in enumerate(z_groups)]
    pool = jnp.concatenate(zl, axis=-1) * scale_ref[...]
    br = _mm(pool.astype(BF16), wbr_ref[...])
    out_ref[...] = (g2.astype(F32) * br).astype(BF16)


def _pool_prompt_body(u_ref, halo_ref, g_ref, lin_ref, scale_ref, wbr_ref, out_ref, *, tm, gd):
    j = pl.program_id(1)
    u = u_ref[...]
    prev = jnp.where(j == 0, 0.0, halo_ref[...])
    sums = _window_sums(jnp.concatenate([prev, u], axis=0))
    pos = j * tm + lax.broadcasted_iota(jnp.int32, (tm, gd), 0)
    zs = []
    for g, wdw in enumerate(POOL_WINDOWS):
        sl = slice(g * gd, (g + 1) * gd)
        cnt = jnp.minimum(pos + 1, wdw).astype(F32)
        zs.append(sums[g][16:, sl] / cnt - u[:, sl])
    _pool_tail(zs, g_ref[...], lin_ref, scale_ref, wbr_ref, out_ref)


def _pool_prompt(u2d, gates, w, batch, seq, tm):
    n, pd = u2d.shape
    d = gates.shape[1] // 2
    ns = seq // tm
    gd = pd // len(POOL_WINDOWS)
    hb = tm // 16
    body = functools.partial(_pool_prompt_body, tm=tm, gd=gd)
    return pl.pallas_call(
        body,
        grid=(batch, ns),
        in_specs=[pl.BlockSpec((tm, pd), lambda b, j: (b * ns + j, 0)),
                  pl.BlockSpec((16, pd), lambda b, j: (jnp.maximum((b * ns + j) * hb - 1, 0), 0)),
                  pl.BlockSpec((tm, d), lambda b, j: (b * ns + j, 1)),
                  _full(w["pool_lin"].shape), _full(w["pool_scale"].shape), _full(w["wpbr"].shape)],
        out_specs=pl.BlockSpec((tm, d), lambda b, j: (b * ns + j, 0)),
        out_shape=jax.ShapeDtypeStruct((n, d), BF16),
        compiler_params=_params(dimension_semantics=("parallel", "parallel")),
        name="pool_prompt",
    )(u2d, u2d, gates, w["pool_lin"], w["pool_scale"], w["wpbr"])


def _pool_sample_body(ext_ref, g_ref, lin_ref, scale_ref, wbr_ref, out_ref, *, nb, rows, n_new, gd):
    ext = ext_ref[...]
    sums = _window_sums(ext)
    pick = lambda a: a.reshape(nb, rows, a.shape[-1])[:, rows - n_new:, :].reshape(nb * n_new, a.shape[-1])
    zs = []
    for g, wdw in enumerate(POOL_WINDOWS):
        sl = slice(g * gd, (g + 1) * gd)
        zs.append(pick(sums[g][:, sl]) / float(wdw) - pick(ext[:, sl]))
    _pool_tail(zs, g_ref[...], lin_ref, scale_ref, wbr_ref, out_ref)


def _pool_sample(ext2d, gates, w, db, rows, n_new):
    pd = ext2d.shape[1]
    d = gates.shape[1] // 2
    gd = pd // len(POOL_WINDOWS)
    nb = _tile(db, 16)
    body = functools.partial(_pool_sample_body, nb=nb, rows=rows, n_new=n_new, gd=gd)
    return pl.pallas_call(
        body,
        grid=(db // nb,),
        in_specs=[pl.BlockSpec((nb * rows, pd), lambda i: (i, 0)), pl.BlockSpec((nb * n_new, d), lambda i: (i, 1)),
                  _full(w["pool_lin"].shape), _full(w["pool_scale"].shape), _full(w["wpbr"].shape)],
        out_specs=pl.BlockSpec((nb * n_new, d), lambda i: (i, 0)),
        out_shape=jax.ShapeDtypeStruct((db * n_new, d), BF16),
        compiler_params=_params(dimension_semantics=("parallel",)),
        name="pool_sample",
    )(ext2d, gates, w["pool_lin"], w["pool_scale"], w["wpbr"])


def _flash_body(q_ref, k_ref, vt_ref, o_ref, m_sc, acc_sc, *, tq, hps, l_row):
    i = pl.program_id(2)
    tv = vt_ref.shape[2]
    m_sc[...] = jnp.full(m_sc.shape, NEG, F32)
    acc_sc[...] = jnp.zeros(acc_sc.shape, F32)

    def chunk(j, masked):
        r0 = pl.multiple_of(j * tq, tq)
        for hh in range(hps):
            sl = slice(hh * HEAD_PAD, (hh + 1) * HEAD_PAD)
            st = _nt(k_ref[pl.ds(r0, tq), sl], q_ref[:, sl])
            if masked:
                kpos = lax.broadcasted_iota(jnp.int32, st.shape, 0)
                qpos = lax.broadcasted_iota(jnp.int32, st.shape, 1)
                st = jnp.where(kpos <= qpos, st, NEG)
            m_old = m_sc[hh]
            m_new = jnp.maximum(m_old, jnp.max(st, axis=0, keepdims=True))
            p = jnp.exp(st - m_new).astype(BF16)
            pv = _mm(vt_ref[j * (tq // tv), sl, :], p[0:tv])
            for c in range(1, tq // tv):
                pv = pv + _mm(vt_ref[j * (tq // tv) + c, sl, :], p[c * tv:(c + 1) * tv])
            acc_sc[hh] = acc_sc[hh] * jnp.exp(m_old - m_new) + pv
            m_sc[hh] = m_new

    def full_chunk(j, carry):
        chunk(j, False)
        return carry

    lax.fori_loop(0, i, full_chunk, 0)
    chunk(i, True)
    for hh in range(hps):
        acc = acc_sc[hh]
        o_ref[:, hh * HEAD_PAD:(hh + 1) * HEAD_PAD] = (acc / acc[l_row:l_row + 1, :]).T.astype(BF16)


def _flash(q, k, vt, batch, seq, n_heads, tq, hps, l_row):
    n = q.shape[0]
    nq = seq // tq
    w = hps * HEAD_PAD
    tv = vt.shape[2]
    body = functools.partial(_flash_body, tq=tq, hps=hps, l_row=l_row)
    return pl.pallas_call(
        body,
        grid=(batch, n_heads // hps, nq),
        in_specs=[pl.BlockSpec((tq, w), lambda b, h, i: (b * nq + i, h)),
                  pl.BlockSpec((seq, w), lambda b, h, i: (b, h)),
                  pl.BlockSpec((seq // tv, w, tv), lambda b, h, i: (b, h, 0))],
        out_specs=pl.BlockSpec((tq, w), lambda b, h, i: (b * nq + i, h)),
        out_shape=jax.ShapeDtypeStruct((n, n_heads * HEAD_PAD), BF16),
        scratch_shapes=[pltpu.VMEM((hps, 1, tq), F32), pltpu.VMEM((hps, HEAD_PAD, tq), F32)],
        compiler_params=_params(dimension_semantics=("parallel", "parallel", "arbitrary")),
        name="flash",
    )(q, k, vt)


def _sattn_body(pt_ref, q_ref, cn_ref, krn_ref, wukt_ref, wukh_ref, knw_ref, wuv_ref, *rest,
                n_pg, pg_chunk, n_heads, n_q, d_nope, d_rope, d_head):
    c_refs = rest[:n_pg]
    kr_refs = rest[n_pg:2 * n_pg]
    o_ref = rest[2 * n_pg]
    wall_sc, qp_sc, m_sc, l_sc, acc_sc = rest[2 * n_pg + 1:]
    s_idx = pl.program_id(1)
    n_kn = n_heads * d_nope
    rows = n_heads * n_q

    @pl.when(s_idx == 0)
    def _():
        knw = knw_ref[...]
        qcs = []
        for h in range(n_heads):
            qp = q_ref[:, h * HEAD_PAD:(h + 1) * HEAD_PAD] * knw
            qp_sc[h * n_q:(h + 1) * n_q, :] = qp
            qcs.append(_nt(qp.astype(BF16), wukh_ref[h]))
        wall_sc[0:n_kn, :] = wukt_ref[...]
        wall_sc[n_kn:n_kn + rows, :] = jnp.concatenate(qcs, axis=0).astype(BF16)
        m_sc[...] = jnp.full(m_sc.shape, NEG, F32)
        l_sc[...] = jnp.zeros(l_sc.shape, F32)
        acc_sc[...] = jnp.zeros(acc_sc.shape, F32)

    def scores(cb, s_rope, krs):
        big = _nt(wall_sc[...], cb)
        knt = big[0:n_kn]
        sq = knt * knt
        ssq = jnp.sum(sq.reshape(d_nope, n_heads, sq.shape[-1]), axis=0) + krs
        r = lax.rsqrt(ssq * (1.0 / d_head) + EPS)
        s_raw = big[n_kn:n_kn + rows] + s_rope
        return jnp.concatenate([s_raw[h * n_q:(h + 1) * n_q, :] * r[h:h + 1, :] for h in range(n_heads)], axis=0)

    def update(s, cb):
        m_old = m_sc[...]
        m_new = jnp.maximum(m_old, jnp.max(s, axis=-1, keepdims=True))
        corr = jnp.exp(m_old - m_new)
        p = jnp.exp(s - m_new)
        l_sc[...] = l_sc[...] * corr + jnp.sum(p, axis=-1, keepdims=True)
        acc_sc[...] = acc_sc[...] * corr + _mm(p.astype(BF16), cb)
        m_sc[...] = m_new

    qr = qp_sc[:, d_nope:d_nope + d_rope].astype(BF16)
    n_chunks = n_pg // pg_chunk
    halves = 2 if n_chunks % 2 == 0 else 1
    per_half = n_chunks // halves
    for hf in range(halves):
        ss, cbs = [], []
        for ck in range(hf * per_half, (hf + 1) * per_half):
            pgs = range(ck * pg_chunk, (ck + 1) * pg_chunk)
            cb = jnp.concatenate([c_refs[p][0, 0] for p in pgs], axis=0).astype(BF16)
            krt = jnp.concatenate([kr_refs[p][0, 0] for p in pgs], axis=1)
            krs = jnp.sum(krt * krt, axis=0, keepdims=True)
            ss.append(scores(cb, _mm(qr, krt.astype(BF16)), krs))
            cbs.append(cb)
        update(jnp.concatenate(ss, axis=1), jnp.concatenate(cbs, axis=0))

    @pl.when(s_idx == pl.num_programs(1) - 1)
    def _():
        cn = cn_ref[0].astype(BF16)
        krn = krn_ref[0]
        kk = krn * krn
        hi = kk.astype(BF16)
        lo = (kk - hi.astype(F32)).astype(BF16)
        ones = jnp.ones((n_heads, kk.shape[1]), BF16)
        krs = _nt(ones, hi) + _nt(ones, lo)
        qpos = lax.broadcasted_iota(jnp.int32, (rows, cn.shape[0]), 0) % n_q
        tpos = lax.broadcasted_iota(jnp.int32, (rows, cn.shape[0]), 1)
        s = scores(cn, _nt(qp_sc[...].astype(BF16), krn.astype(BF16)), krs)
        update(jnp.where(tpos <= qpos, s, NEG), cn)
        lat = (acc_sc[...] / l_sc[...]).astype(BF16)
        for h in range(n_heads):
            sl = slice(h * HEAD_PAD, (h + 1) * HEAD_PAD)
            o_ref[:, sl] = _mm(lat[h * n_q:(h + 1) * n_q, :], wuv_ref[:, sl])


def _sattn(page_table, q, c_new_pad, kr_new_pad, ckv, krope_t, layer, w, dims, n_pg):
    db, n_pages = page_table.shape
    n_q = q.shape[0] // db
    page, kv_lora = ckv.shape[2], ckv.shape[3]
    d_rope = krope_t.shape[2]
    n_heads = dims["n_heads"]
    hp = n_heads * HEAD_PAD
    steps = n_pages // n_pg
    rows = n_heads * n_q
    n_kn = n_heads * dims["d_nope"]
    pg_chunk = 4 if n_pg % 4 == 0 else 1
    body = functools.partial(_sattn_body, n_pg=n_pg, pg_chunk=pg_chunk, n_heads=n_heads, n_q=n_q,
                             d_nope=dims["d_nope"], d_rope=d_rope, d_head=dims["d_head"])
    pt = page_table.reshape(-1)

    def page_map(p):
        return lambda b, s, pt_ref: (layer, pt_ref[b * n_pages + s * n_pg + p], 0, 0)

    fixed = lambda shape: pl.BlockSpec(shape, lambda b, s, pt_ref: (0,) * len(shape))
    grid_spec = pltpu.PrefetchScalarGridSpec(
        num_scalar_prefetch=1,
        grid=(db, steps),
        in_specs=[pl.BlockSpec((n_q, hp), lambda b, s, pt_ref: (b, 0)),
                  pl.BlockSpec((1, page, kv_lora), lambda b, s, pt_ref: (b, 0, 0)),
                  pl.BlockSpec((1, page, HEAD_PAD), lambda b, s, pt_ref: (b, 0, 0)),
                  fixed(w["wukt"].shape), fixed(w["wukh"].shape), fixed(w["knw_raw"].shape), fixed(w["wuv"].shape)]
                 + [pl.BlockSpec((1, 1, page, kv_lora), page_map(p)) for p in range(n_pg)]
                 + [pl.BlockSpec((1, 1, d_rope, page), page_map(p)) for p in range(n_pg)],
        out_specs=pl.BlockSpec((n_q, hp), lambda b, s, pt_ref: (b, 0)),
        scratch_shapes=[pltpu.VMEM((n_kn + rows, kv_lora), BF16), pltpu.VMEM((rows, HEAD_PAD), F32),
                        pltpu.VMEM((rows, 1), F32), pltpu.VMEM((rows, 1), F32), pltpu.VMEM((rows, kv_lora), F32)],
    )
    return pl.pallas_call(
        body,
        grid_spec=grid_spec,
        out_shape=jax.ShapeDtypeStruct((db * n_q, hp), F32),
        compiler_params=_params(dimension_semantics=("parallel", "arbitrary")),
        name="sattn",
    )(pt, q, c_new_pad, kr_new_pad, w["wukt"], w["wukh"], w["knw_raw"], w["wuv"],
      *([ckv] * n_pg), *([krope_t] * n_pg))


def _post_body(x_ref, a_ref, g_ref, pc_ref, wab_ref, wout_ref, h_ref):
    br = _mm(a_ref[...].astype(BF16), wab_ref[...])
    mixed = g_ref[...].astype(F32) * br + pc_ref[...].astype(F32)
    h_ref[...] = x_ref[...] + _mm(mixed.astype(BF16), wout_ref[...])


def _post(x2d, attn, gates, pc, w, tm):
    n, d = x2d.shape
    row = lambda i: (i, 0)
    return pl.pallas_call(
        _post_body,
        grid=(n // tm,),
        in_specs=[pl.BlockSpec((tm, d), row), pl.BlockSpec((tm, attn.shape[1]), row), pl.BlockSpec((tm, d), row),
                  pl.BlockSpec((tm, d), row), _full(w["wab"].shape), _full(w["wout"].shape)],
        out_specs=pl.BlockSpec((tm, d), row),
        out_shape=jax.ShapeDtypeStruct((n, d), F32),
        compiler_params=_params(dimension_semantics=("parallel",)),
        name="post",
    )(x2d, attn, gates, pc, w["wab"], w["wout"])


def _gelu(x):
    k = -2.0 * math.sqrt(2.0 / math.pi) * math.log2(math.e)
    return x / (1.0 + jnp.exp2(x * (k + (k * 0.044715) * (x * x))))


def _batcher_pairs(n):
    pairs = []
    p = 1
    while p < n:
        k = p
        while k >= 1:
            for j in range(k % p, n - k, 2 * k):
                for i in range(min(k, n - j - k)):
                    if (i + j) // (2 * p) == (i + j + k) // (2 * p):
                        pairs.append((i + j, i + j + k))
            k //= 2
        p *= 2
    return pairs


def _top_merge(s, with_rank=False):
    n = PEER_TOPK
    assert s.shape[0] == 8 * n
    st = [s[8 * v:8 * v + 8] for v in range(n)]
    for a, b in _batcher_pairs(n):
        st[a], st[b] = jnp.maximum(st[a], st[b]), jnp.minimum(st[a], st[b])
    for shift in (4, 2, 1):
        other = [pltpu.roll(x, shift, axis=0) for x in st]
        st = [jnp.maximum(st[i], other[n - 1 - i]) for i in range(n)]
        stride = n // 2
        while stride >= 1:
            for i in range(n):
                if (i // stride) % 2 == 0:
                    j = i + stride
                    st[i], st[j] = jnp.maximum(st[i], st[j]), jnp.minimum(st[i], st[j])
            stride //= 2
    rid = lax.broadcasted_iota(jnp.int32, st[0].shape, 0)
    halves = []
    for half in range(n // 8):
        acc = st[half * 8]
        for a in range(1, 8):
            acc = jnp.where(rid == a, st[half * 8 + a], acc)
        halves.append(acc)
    tops = jnp.concatenate(halves, axis=0)
    if not with_rank:
        return tops
    rank = jnp.zeros(s.shape, F32)
    for b in range(n):
        rank = rank + jnp.where(tops[b:b + 1] > s, 1.0, 0.0)
    return tops, rank


def _top_rows(s, k, with_rank=False):
    r, t = s.shape
    groups = r // 8
    st = [s[8 * v:8 * v + 8] for v in range(groups)]
    for a, b in _batcher_pairs(1 << (groups - 1).bit_length()):
        if b < groups:
            st[a], st[b] = jnp.maximum(st[a], st[b]), jnp.minimum(st[a], st[b])
    rid = lax.broadcasted_iota(jnp.int32, (k, t), 0)
    ninf = jnp.full((1, 8, t), -jnp.inf, F32)

    def step(i, carry):
        stack, tops = carry
        m = jnp.max(stack[0], axis=0, keepdims=True)
        tops = jnp.where(rid == i, m, tops)
        hit = stack[0] == m
        stack = jnp.where(hit[None], jnp.concatenate([stack[1:], ninf], axis=0), stack)
        return stack, tops

    _, tops = lax.fori_loop(0, k, step, (jnp.stack(st), jnp.full((k, t), -jnp.inf, F32)))
    if not with_rank:
        return tops
    rank = jnp.zeros(s.shape, F32)
    for b in range(k):
        rank = rank + jnp.where(tops[b:b + 1] > s, 1.0, 0.0)
    return tops, rank


def _peer_body(h_ref, n2_ref, wqt_ref, keys_ref, u_ref, vt_ref, y_ref,
               hn_sc, qt_sc, r1_sc, e1_sc, b_sc, e0_sc, g_sc, acc_sc,
               *, n_heads, n_keys, irows):
    s_idx = pl.program_id(1)
    k = PEER_TOPK
    t = hn_sc.shape[1]

    @pl.when(s_idx == 0)
    def _():
        hn_sc[...] = _rms(h_ref[...], n2_ref[...]).T.astype(BF16)
        acc_sc[...] = jnp.zeros(acc_sc.shape, F32)
        qrows = qt_sc.shape[0] // 4
        for c in range(4):
            qt_sc[c * qrows:(c + 1) * qrows, :] = _mm(wqt_ref[c * qrows:(c + 1) * qrows, :], hn_sc[...]).astype(BF16)

        def head(h, _):
            def scores(p):
                r0 = pl.multiple_of((h * 2 + p) * n_keys, n_keys)
                return _mm(keys_ref[h * 2 + p], qt_sc[pl.ds(r0, n_keys), :])

            s0 = scores(0)
            s1 = scores(1)
            top0 = _top_merge(s0)
            top1, rank1 = _top_merge(s1, with_rank=True)
            cand = jnp.concatenate(
                [top0[0:1] + top1]
                + [top0[a:a + 1] + top1[0:8] for a in range(1, 8)]
                + [top0[8:16] + top1[0:1]], axis=0)
            ctop = _top_rows(cand, k + 1)
            c_k = ctop[k - 1:k]
            tau = c_k + 0.5 * (ctop[k:k + 1] - c_k)
            m = top0[0:1] + top1[0:1]
            z = jnp.sum(jnp.where(cand >= c_k, jnp.exp(cand - m), 0.0), axis=0, keepdims=True)
            in0 = s0 >= top0[k - 1:k]
            cnt = jnp.zeros(s0.shape, F32)
            for b in range(k):
                cnt = cnt + jnp.where(s0 + top1[b:b + 1] >= tau, 1.0, 0.0)
            b_sc[h] = jnp.where(in0, cnt, 0.0)
            e0_sc[h] = jnp.where(in0, jnp.exp(s0 - top0[0:1]) / z, 0.0)
            e1_sc[h] = jnp.exp(s1 - top1[0:1]).astype(BF16)
            r1_sc[h] = rank1.astype(BF16)
            return 0

        lax.fori_loop(0, n_heads, head, 0)

    base = pl.multiple_of(s_idx * irows, irows)
    for lt in range(t // LANES):
        ls = slice(lt * LANES, (lt + 1) * LANES)
        b_t = [b_sc[h, pl.ds(base, irows), ls] for h in range(n_heads)]
        e0_t = [e0_sc[h, pl.ds(base, irows), ls] for h in range(n_heads)]
        for ii in range(irows):
            wgt = jnp.zeros((n_keys, LANES), BF16)
            for h in range(n_heads):
                sel = r1_sc[h, :, ls] < b_t[h][ii:ii + 1, :].astype(BF16)
                wgt = wgt + jnp.where(sel, e1_sc[h, :, ls] * e0_t[h][ii:ii + 1, :].astype(BF16),
                                      jnp.zeros((), BF16))
            g_sc[ii * n_keys:(ii + 1) * n_keys, ls] = wgt
    act = _gelu(_mm(u_ref[...], hn_sc[...])).astype(BF16)
    acc_sc[...] += _mm(vt_ref[0], g_sc[...] * act)

    @pl.when(s_idx == pl.num_programs(1) - 1)
    def _():
        y_ref[...] = h_ref[...] + acc_sc[...].T


def _peer(h2d, w, tt, et):
    n, d = h2d.shape
    n_heads, n_keys = w["n_peer_heads"], w["n_keys"]
    n_exp = w["pu"].shape[0]
    irows = et // n_keys
    n_blk = n_exp // et
    body = functools.partial(_peer_body, n_heads=n_heads, n_keys=n_keys, irows=irows)
    small32 = pltpu.VMEM((n_heads, n_keys, tt), F32)
    small16 = pltpu.VMEM((n_heads, n_keys, tt), BF16)
    return pl.pallas_call(
        body,
        grid=(n // tt, n_blk),
        in_specs=[pl.BlockSpec((tt, d), lambda i, s: (i, 0)), _full(w["n2"].shape), _full(w["wqt"].shape),
                  _full(w["keys"].shape),
                  pl.BlockSpec((et, d), lambda i, s: (s, 0)),
                  pl.BlockSpec((1, d, et), lambda i, s: (s, 0, 0))],
        out_specs=pl.BlockSpec((tt, d), lambda i, s: (i, 0)),
        out_shape=jax.ShapeDtypeStruct((n, d), F32),
        scratch_shapes=[pltpu.VMEM((d, tt), BF16), pltpu.VMEM((w["wqt"].shape[0], tt), BF16),
                        small16, small16, small32, small32,
                        pltpu.VMEM((et, tt), BF16), pltpu.VMEM((d, tt), F32)],
        compiler_params=_params(dimension_semantics=("parallel", "arbitrary")),
        name="peer",
    )(h2d, w["n2"], w["wqt"], w["keys"], w["pu"], w["pvt"])


def _rot_cols(wr):
    half = wr.shape[-1] // 2
    return jnp.concatenate([-wr[..., half:], wr[..., :half]], axis=-1)


def _rope_tables(pos, d_nope, d_rope):
    half = d_rope // 2
    inv_freq = ROPE_THETA ** (-jnp.arange(half, dtype=F32) / half)
    ang = pos.astype(F32)[:, None] * inv_freq[None, :]
    cos = jnp.cos(ang)
    sin = jnp.sin(ang)
    n = pos.shape[0]
    pad = jnp.zeros((n, HEAD_PAD - d_nope - d_rope), F32)
    ta = jnp.concatenate([jnp.ones((n, d_nope), F32), cos, cos, pad], axis=-1)
    tb = jnp.concatenate([jnp.zeros((n, d_nope), F32), sin, sin, pad], axis=-1)
    return ta, tb


def _layer_weights(l, norm1_w, w_in, q_lat_norm_w, w_q_up, kv_lat_norm_w, w_uk, w_uv, q_norm_w, k_norm_w,
                   w_attn_br, pool_lin_w, pool_scale, w_pool_br, w_out, norm2_w, peer_w_q, peer_keys,
                   peer_u, peer_v, d_rope):
    d = w_in.shape[1]
    q_lora = w_q_up.shape[1]
    n_heads, d_head = w_q_up.shape[2], w_q_up.shape[3]
    kv_lora = w_uk.shape[1]
    d_nope = w_uk.shape[3]
    d_v = w_uv.shape[3]
    off_kv = q_lora
    off_kr = off_kv + kv_lora
    off_pool = off_kr + d_rope
    scale = d_head ** -0.5
    wi = w_in[l]
    w_kr = wi[:, off_kr:off_pool]
    krblk = jnp.concatenate([jnp.zeros((d, d_nope), F32), w_kr, _rot_cols(w_kr)], axis=-1)
    win = jnp.concatenate([wi[:, :off_kr], krblk, wi[:, off_pool:]], axis=-1).astype(BF16)
    wq = w_q_up[l]
    wq = jnp.concatenate([wq[..., :d_nope], wq[..., d_nope:], _rot_cols(wq[..., d_nope:])], axis=-1)
    wq = wq.reshape(q_lora, n_heads * HEAD_PAD).astype(BF16)
    padk = lambda a, width: jnp.concatenate([a, jnp.zeros(a.shape[:-1] + (HEAD_PAD - width,), F32)], axis=-1)
    wuk_pad = padk(w_uk[l], d_nope)
    wuv_pad = padk(w_uv[l], d_v)
    wab = jnp.concatenate([w_attn_br[l].reshape(n_heads, d_v, d),
                           jnp.zeros((n_heads, HEAD_PAD - d_v, d), F32)], axis=1).reshape(n_heads * HEAD_PAD, d)
    n_ph, _, n_keys, half = peer_keys.shape[1:]
    return dict(
        n1=norm1_w[l][None, :], win=win, qlw=q_lat_norm_w[l][None, :], wq=wq, kvw=kv_lat_norm_w[l][None, :],
        wuk=wuk_pad.reshape(kv_lora, n_heads * HEAD_PAD).astype(BF16),
        wuv=wuv_pad.reshape(kv_lora, n_heads * HEAD_PAD).astype(BF16),
        wuvt=wuv_pad.reshape(kv_lora, n_heads * HEAD_PAD).T.astype(BF16),
        qnw=padk(q_norm_w[l], d_head)[None, :] * scale, knw=padk(k_norm_w[l], d_head)[None, :],
        knw_raw=padk(k_norm_w[l], d_head)[None, :],
        wukt=jnp.transpose(w_uk[l], (2, 1, 0)).reshape(d_nope * n_heads, kv_lora).astype(BF16),
        wukh=jnp.transpose(wuk_pad, (1, 0, 2)).astype(BF16),
        wab=wab.astype(BF16), wout=w_out[l].astype(BF16),
        pool_lin=pool_lin_w[l].astype(BF16), pool_scale=pool_scale[l][None, :], wpbr=w_pool_br[l].astype(BF16),
        n2=norm2_w[l][None, :], wqt=peer_w_q[l].T.astype(BF16),
        keys=peer_keys[l].reshape(n_ph * 2, n_keys, half).astype(BF16),
        pu=peer_u[l].astype(BF16),
        n_peer_heads=n_ph, n_keys=n_keys,
    ), dict(q_lora=q_lora, kv_lora=kv_lora, pool_dim=off_pool_dim(wi, off_pool, d), n_heads=n_heads,
            d_head=d_head, d_nope=d_nope, d_v=d_v)


def off_pool_dim(wi, off_pool, d):
    return wi.shape[1] - off_pool - 2 * d


def _tile(n, pref):
    t = pref
    while n % t:
        t //= 2
    return t


def kernel(x_prompt, x_sample, cache_ckv, cache_krope, state_pool, page_table, norm1_w, w_in, q_lat_norm_w, w_q_up, kv_lat_norm_w, w_uk, w_uv, q_norm_w, k_norm_w, w_attn_br, pool_lin_w, pool_scale, w_pool_br, w_out, norm2_w, peer_w_q, peer_keys, peer_u, peer_v):
    batch, seq, d = x_prompt.shape
    db, n_q, _ = x_sample.shape
    depth = w_in.shape[0]
    page = cache_ckv.shape[2]
    d_rope = cache_krope.shape[3]
    n_pages = page_table.shape[1]
    past_len = n_pages * page
    n_state = state_pool.shape[2]
    n_p, n_s = batch * seq, db * n_q
    xp = x_prompt.reshape(n_p, d)
    xs = x_sample.reshape(n_s, d)
    outs = [[] for _ in range(6)]
    for l in range(depth):
        w, dims = _layer_weights(l, norm1_w, w_in, q_lat_norm_w, w_q_up, kv_lat_norm_w, w_uk, w_uv, q_norm_w,
                                 k_norm_w, w_attn_br, pool_lin_w, pool_scale, w_pool_br, w_out, norm2_w,
                                 peer_w_q, peer_keys, peer_u, peer_v, d_rope)
        d_nope = dims["d_nope"]
        kv_lora = dims["kv_lora"]
        pool_dim = dims["pool_dim"]
        rope_sl = slice(d_nope, d_nope + d_rope)
        tm_p = _tile(seq, 512)
        tm_s = _tile(n_s, 256)
        tt_p = _tile(n_p, 512)
        tt_s = _tile(n_s, 512)
        et = _tile(w["pu"].shape[0], 2048)
        w["pvt"] = jnp.swapaxes(peer_v[l].reshape(-1, et, d), 1, 2).astype(BF16)

        ta, tb = _rope_tables(jnp.arange(seq), d_nope, d_rope)
        q, k, v, c, krp, u, gates = _proj(xp, ta, tb, w, dims, tm_p)
        attn = _flash(q, k, v, batch, seq, dims["n_heads"], _tile(seq, 512), _tile(dims["n_heads"], 8),
                      dims["d_v"])
        pc = _pool_prompt(u, gates, w, batch, seq, tm_p)
        h_p = _post(xp, attn, gates, pc, w, tt_p)
        outs[0].append(c.reshape(batch, seq, kv_lora))
        outs[1].append(krp[:, rope_sl].reshape(batch, seq, d_rope))
        outs[2].append(u.reshape(batch, seq, pool_dim)[:, seq - n_state:])

        pos_s = past_len + jnp.arange(n_q)
        ta, tb = _rope_tables(jnp.tile(pos_s, tm_s // n_q), d_nope, d_rope)
        q, _, _, c, krp, u, gates = _proj(xs, ta, tb, w, dims, tm_s)
        zrow = lambda a, rows: jnp.concatenate(
            [a, jnp.zeros((a.shape[0], rows - a.shape[1], a.shape[2]), a.dtype)], axis=1)
        c_new = zrow(c.reshape(db, n_q, kv_lora), page)
        kr_new = zrow(krp.reshape(db, n_q, HEAD_PAD), page)
        n_pg = _tile(n_pages, 64)
        krope_t = jnp.swapaxes(cache_krope, 2, 3)
        attn = _sattn(page_table, q.astype(F32), c_new, kr_new, cache_ckv, krope_t, l, w, dims, n_pg)
        u3 = u.reshape(db, n_q, pool_dim)
        u_full = jnp.concatenate([state_pool[l], u3], axis=1)
        rows = -(-(n_state + n_q + 1) // 8) * 8
        ext = jnp.concatenate([jnp.zeros((db, rows - n_state - n_q, pool_dim), F32), u_full], axis=1)
        pc = _pool_sample(ext.reshape(db * rows, pool_dim), gates, w, db, rows, n_q)
        h_s = _post(xs, attn, gates, pc, w, tt_s)
        xp = _peer(h_p, w, tt_p, et)
        xs = _peer(h_s, w, tt_s, et)
        outs[3].append(c.reshape(db, n_q, kv_lora))
        outs[4].append(krp[:, rope_sl].reshape(db, n_q, d_rope))
        outs[5].append(u_full[:, -n_state:])
    return (xp.reshape(batch, seq, d), xs.reshape(db, n_q, d)) + tuple(jnp.stack(o) for o in outs)
```

```python
import functools
import math

import jax
import jax.numpy as jnp
from jax import lax
from jax.experimental import pallas as pl
from jax.experimental.pallas import tpu as pltpu

F32 = jnp.float32
BF16 = jnp.bfloat16

EPS = 1e-6
ROPE_THETA = 10000.0
POOL_WINDOWS = (2, 4, 8, 16)
PEER_TOPK = 16
NEG = -1e30
LANES = 128
HEAD_PAD = 128
VMEM_LIMIT = 56 * 1024 * 1024


def _nt(a, b):
    return lax.dot_general(a, b, (((1,), (1,)), ((), ())), preferred_element_type=F32)


def _mm(a, b):
    return jnp.dot(a, b, preferred_element_type=F32)


def _rms(x, w):
    return x * lax.rsqrt(jnp.mean(x * x, axis=-1, keepdims=True) + EPS) * w


def _params(**kw):
    return pltpu.CompilerParams(vmem_limit_bytes=VMEM_LIMIT, **kw)


def _full(shape):
    n = len(shape)
    return pl.BlockSpec(shape, lambda *_: (0,) * n)


def _head_norm(blk, ta, tb, w, d_head, d_rope):
    rot = pltpu.roll(blk, HEAD_PAD - d_rope, axis=1)
    hq = blk * ta + rot * tb
    ssq = jnp.sum(hq * hq, axis=-1, keepdims=True)
    return hq * lax.rsqrt(ssq * (1.0 / d_head) + EPS) * w


def _proj_body(x_ref, ta_ref, tb_ref, n1_ref, win_ref, qlw_ref, wq_ref, kvw_ref, wuk_ref, wuv_ref,
               qnw_ref, knw_ref, q_ref, k_ref, v_ref, c_ref, kr_ref, u_ref, g_ref,
               *, q_lora, kv_lora, pool_dim, n_heads, d_head, d_v, d_rope):
    x = x_ref[...]
    xn = _rms(x, n1_ref[...]).astype(BF16)
    proj = _mm(xn, win_ref[...])
    o1 = q_lora
    o2 = o1 + kv_lora
    o3 = o2 + HEAD_PAD
    o4 = o3 + pool_dim
    ta = ta_ref[...]
    tb = tb_ref[...]

    ql = _rms(proj[:, :o1], qlw_ref[...]).astype(BF16)
    qu = _mm(ql, wq_ref[...])
    c = _rms(proj[:, o1:o2], kvw_ref[...])
    c_ref[...] = c
    cb = c.astype(BF16)
    krb = proj[:, o2:o3]
    krp = krb * ta + pltpu.roll(krb, HEAD_PAD - d_rope, axis=1) * tb
    kr_ref[...] = krp
    ku = _mm(cb, wuk_ref[...])
    vt = _nt(wuv_ref[...], cb)
    vrow = lax.broadcasted_iota(jnp.int32, vt.shape, 0) % HEAD_PAD
    v_ref[0] = jnp.where(vrow == d_v, 1.0, vt).astype(BF16)
    qnw = qnw_ref[...]
    knw = knw_ref[...]
    for h in range(n_heads):
        sl = slice(h * HEAD_PAD, (h + 1) * HEAD_PAD)
        q_ref[:, sl] = _head_norm(qu[:, sl], ta, tb, qnw, d_head, d_rope).astype(BF16)
        kh = ku[:, sl] + krp
        ssq = jnp.sum(kh * kh, axis=-1, keepdims=True)
        k_ref[:, sl] = (kh * lax.rsqrt(ssq * (1.0 / d_head) + EPS) * knw).astype(BF16)
    u_ref[...] = proj[:, o3:o4]
    g_ref[...] = jax.nn.sigmoid(proj[:, o4:]).astype(BF16)


def _proj(x2d, ta, tb, w, dims, tm):
    n, d = x2d.shape
    nt = n // tm
    tab_blocks = ta.shape[0] // tm
    row = lambda i: (i, 0)
    tab = lambda i: (i % tab_blocks, 0)
    hp = dims["n_heads"] * HEAD_PAD
    body = functools.partial(_proj_body, q_lora=dims["q_lora"], kv_lora=dims["kv_lora"],
                             pool_dim=dims["pool_dim"], n_heads=dims["n_heads"], d_head=dims["d_head"],
                             d_v=dims["d_v"], d_rope=dims["d_rope"])
    consts = [w["n1"], w["win"], w["qlw"], w["wq"], w["kvw"], w["wuk"], w["wuvt"], w["qnw"], w["knw"]]
    return pl.pallas_call(
        body,
        grid=(nt,),
        in_specs=[pl.BlockSpec((tm, d), row), pl.BlockSpec((tm, HEAD_PAD), tab), pl.BlockSpec((tm, HEAD_PAD), tab)]
                 + [_full(a.shape) for a in consts],
        out_specs=[pl.BlockSpec((tm, hp), row), pl.BlockSpec((tm, hp), row),
                   pl.BlockSpec((1, hp, tm), lambda i: (i, 0, 0)),
                   pl.BlockSpec((tm, dims["kv_lora"]), row), pl.BlockSpec((tm, HEAD_PAD), row),
                   pl.BlockSpec((tm, dims["pool_dim"]), row), pl.BlockSpec((tm, 2 * d), row)],
        out_shape=[jax.ShapeDtypeStruct((n, hp), BF16), jax.ShapeDtypeStruct((n, hp), BF16),
                   jax.ShapeDtypeStruct((nt, hp, tm), BF16), jax.ShapeDtypeStruct((n, dims["kv_lora"]), F32),
                   jax.ShapeDtypeStruct((n, HEAD_PAD), F32), jax.ShapeDtypeStruct((n, dims["pool_dim"]), F32),
                   jax.ShapeDtypeStruct((n, 2 * d), BF16)],
        compiler_params=_params(dimension_semantics=("parallel",)),
        name="proj",
    )(x2d, ta, tb, *consts)


def _window_sums(ext):
    s2 = ext + pltpu.roll(ext, 1, axis=0)
    s4 = s2 + pltpu.roll(s2, 2, axis=0)
    s8 = s4 + pltpu.roll(s4, 4, axis=0)
    s16 = s8 + pltpu.roll(s8, 8, axis=0)
    return (s2, s4, s8, s16)


def _pool_tail(z_groups, g2, lin_ref, scale_ref, wbr_ref, out_ref):
    zl = [_mm(z.astype(BF16), lin_ref[g]) for g, z in enumerate(z_groups)]
    pool = jnp.concatenate(zl, axis=-1) * scale_ref[...]
    br = _mm(pool.astype(BF16), wbr_ref[...])
    out_ref[...] = (g2.astype(F32) * br).astype(BF16)


def _pool_prompt_body(u_ref, halo_ref, g_ref, lin_ref, scale_ref, wbr_ref, out_ref, *, tm, gd):
    j = pl.program_id(1)
    u = u_ref[...]
    prev = jnp.where(j == 0, 0.0, halo_ref[...])
    sums = _window_sums(jnp.concatenate([prev, u], axis=0))
    pos = j * tm + lax.broadcasted_iota(jnp.int32, (tm, gd), 0)
    zs = []
    for g, wdw in enumerate(POOL_WINDOWS):
        sl = slice(g * gd, (g + 1) * gd)
        cnt = jnp.minimum(pos + 1, wdw).astype(F32)
        zs.append(sums[g][16:, sl] / cnt - u[:, sl])
    _pool_tail(zs, g_ref[...], lin_ref, scale_ref, wbr_ref, out_ref)


def _pool_prompt(u2d, gates, w, batch, seq, tm):
    n, pd = u2d.shape
    d = gates.shape[1] // 2
    ns = seq // tm
    gd = pd // len(POOL_WINDOWS)
    hb = tm // 16
    body = functools.partial(_pool_prompt_body, tm=tm, gd=gd)
    return pl.pallas_call(
        body,
        grid=(batch, ns),
        in_specs=[pl.BlockSpec((tm, pd), lambda b, j: (b * ns + j, 0)),
                  pl.BlockSpec((16, pd), lambda b, j: (jnp.maximum((b * ns + j) * hb - 1, 0), 0)),
                  pl.BlockSpec((tm, d), lambda b, j: (b * ns + j, 1)),
                  _full(w["pool_lin"].shape), _full(w["pool_scale"].shape), _full(w["wpbr"].shape)],
        out_specs=pl.BlockSpec((tm, d), lambda b, j: (b * ns + j, 0)),
        out_shape=jax.ShapeDtypeStruct((n, d), BF16),
        compiler_params=_params(dimension_semantics=("parallel", "parallel")),
        name="pool_prompt",
    )(u2d, u2d, gates, w["pool_lin"], w["pool_scale"], w["wpbr"])


def _pool_sample_body(ext_ref, g_ref, lin_ref, scale_ref, wbr_ref, out_ref, *, nb, rows, n_new, gd):
    ext = ext_ref[...]
    sums = _window_sums(ext)
    pick = lambda a: a.reshape(nb, rows, a.shape[-1])[:, rows - n_new:, :].reshape(nb * n_new, a.shape[-1])
    zs = []
    for g, wdw in enumerate(POOL_WINDOWS):
        sl = slice(g * gd, (g + 1) * gd)
        zs.append(pick(sums[g][:, sl]) / float(wdw) - pick(ext[:, sl]))
    _pool_tail(zs, g_ref[...], lin_ref, scale_ref, wbr_ref, out_ref)


def _pool_sample(ext2d, gates, w, db, rows, n_new):
    pd = ext2d.shape[1]
    d = gates.shape[1] // 2
    gd = pd // len(POOL_WINDOWS)
    nb = _tile(db, 16)
    body = functools.partial(_pool_sample_body, nb=nb, rows=rows, n_new=n_new, gd=gd)
    return pl.pallas_call(
        body,
        grid=(db // nb,),
        in_specs=[pl.BlockSpec((nb * rows, pd), lambda i: (i, 0)), pl.BlockSpec((nb * n_new, d), lambda i: (i, 1)),
                  _full(w["pool_lin"].shape), _full(w["pool_scale"].shape), _full(w["wpbr"].shape)],
        out_specs=pl.BlockSpec((nb * n_new, d), lambda i: (i, 0)),
        out_shape=jax.ShapeDtypeStruct((db * n_new, d), BF16),
        compiler_params=_params(dimension_semantics=("parallel",)),
        name="pool_sample",
    )(ext2d, gates, w["pool_lin"], w["pool_scale"], w["wpbr"])


def _flash_body(q_ref, k_ref, vt_ref, o_ref, m_sc, acc_sc, *, tq, hps, l_row):
    i = pl.program_id(2)
    tv = vt_ref.shape[2]
    m_sc[...] = jnp.full(m_sc.shape, NEG, F32)
    acc_sc[...] = jnp.zeros(acc_sc.shape, F32)

    def chunk(j, masked):
        r0 = pl.multiple_of(j * tq, tq)
        for hh in range(hps):
            sl = slice(hh * HEAD_PAD, (hh + 1) * HEAD_PAD)
            st = _nt(k_ref[pl.ds(r0, tq), sl], q_ref[:, sl])
            if masked:
                kpos = lax.broadcasted_iota(jnp.int32, st.shape, 0)
                qpos = lax.broadcasted_iota(jnp.int32, st.shape, 1)
                st = jnp.where(kpos <= qpos, st, NEG)
            m_old = m_sc[hh]
            m_new = jnp.maximum(m_old, jnp.max(st, axis=0, keepdims=True))
            p = jnp.exp(st - m_new).astype(BF16)
            pv = _mm(vt_ref[j * (tq // tv), sl, :], p[0:tv])
            for c in range(1, tq // tv):
                pv = pv + _mm(vt_ref[j * (tq // tv) + c, sl, :], p[c * tv:(c + 1) * tv])
            acc_sc[hh] = acc_sc[hh] * jnp.exp(m_old - m_new) + pv
            m_sc[hh] = m_new

    def full_chunk(j, carry):
        chunk(j, False)
        return carry

    lax.fori_loop(0, i, full_chunk, 0)
    chunk(i, True)
    for hh in range(hps):
        acc = acc_sc[hh]
        o_ref[:, hh * HEAD_PAD:(hh + 1) * HEAD_PAD] = (acc / acc[l_row:l_row + 1, :]).T.astype(BF16)


def _flash(q, k, vt, batch, seq, n_heads, tq, hps, l_row):
    n = q.shape[0]
    nq = seq // tq
    w = hps * HEAD_PAD
    tv = vt.shape[2]
    body = functools.partial(_flash_body, tq=tq, hps=hps, l_row=l_row)
    return pl.pallas_call(
        body,
        grid=(batch, n_heads // hps, nq),
        in_specs=[pl.BlockSpec((tq, w), lambda b, h, i: (b * nq + i, h)),
                  pl.BlockSpec((seq, w), lambda b, h, i: (b, h)),
                  pl.BlockSpec((seq // tv, w, tv), lambda b, h, i: (b, h, 0))],
        out_specs=pl.BlockSpec((tq, w), lambda b, h, i: (b * nq + i, h)),
        out_shape=jax.ShapeDtypeStruct((n, n_heads * HEAD_PAD), BF16),
        scratch_shapes=[pltpu.VMEM((hps, 1, tq), F32), pltpu.VMEM((hps, HEAD_PAD, tq), F32)],
        compiler_params=_params(dimension_semantics=("parallel", "parallel", "arbitrary")),
        name="flash",
    )(q, k, vt)


def _sattn_body(pt_ref, q_ref, cn_ref, krn_ref, wukt_ref, wukh_ref, knw_ref, wuv_ref, *rest,
                n_pg, pg_chunk, n_heads, n_q, d_nope, d_rope, d_head):
    c_refs = rest[:n_pg]
    kr_refs = rest[n_pg:2 * n_pg]
    o_ref = rest[2 * n_pg]
    wall_sc, qp_sc, m_sc, l_sc, acc_sc = rest[2 * n_pg + 1:]
    s_idx = pl.program_id(1)
    n_kn = n_heads * d_nope
    rows = n_heads * n_q

    @pl.when(s_idx == 0)
    def _():
        knw = knw_ref[...]
        qcs = []
        for h in range(n_heads):
            qp = q_ref[:, h * HEAD_PAD:(h + 1) * HEAD_PAD] * knw
            qp_sc[h * n_q:(h + 1) * n_q, :] = qp
            qcs.append(_nt(qp.astype(BF16), wukh_ref[h]))
        wall_sc[0:n_kn, :] = wukt_ref[...]
        wall_sc[n_kn:n_kn + rows, :] = jnp.concatenate(qcs, axis=0).astype(BF16)
        m_sc[...] = jnp.full(m_sc.shape, NEG, F32)
        l_sc[...] = jnp.zeros(l_sc.shape, F32)
        acc_sc[...] = jnp.zeros(acc_sc.shape, F32)

    def scores(cb, s_rope, krs):
        big = _nt(wall_sc[...], cb)
        knt = big[0:n_kn]
        sq = knt * knt
        ssq = jnp.sum(sq.reshape(d_nope, n_heads, sq.shape[-1]), axis=0) + krs
        r = lax.rsqrt(ssq * (1.0 / d_head) + EPS)
        s_raw = big[n_kn:n_kn + rows] + s_rope
        return jnp.concatenate([s_raw[h * n_q:(h + 1) * n_q, :] * r[h:h + 1, :] for h in range(n_heads)], axis=0)

    def update(s, cb):
        m_old = m_sc[...]
        m_new = jnp.maximum(m_old, jnp.max(s, axis=-1, keepdims=True))
        corr = jnp.exp(m_old - m_new)
        p = jnp.exp(s - m_new)
        l_sc[...] = l_sc[...] * corr + jnp.sum(p, axis=-1, keepdims=True)
        acc_sc[...] = acc_sc[...] * corr + _mm(p.astype(BF16), cb)
        m_sc[...] = m_new

    qr = qp_sc[:, d_nope:d_nope + d_rope].astype(BF16)
    n_chunks = n_pg // pg_chunk
    halves = 2 if n_chunks % 2 == 0 else 1
    per_half = n_chunks // halves
    for hf in range(halves):
        ss, cbs = [], []
        for ck in range(hf * per_half, (hf + 1) * per_half):
            pgs = range(ck * pg_chunk, (ck + 1) * pg_chunk)
            cb = jnp.concatenate([c_refs[p][0, 0] for p in pgs], axis=0).astype(BF16)
            krt = jnp.concatenate([kr_refs[p][0, 0] for p in pgs], axis=1)
            krs = jnp.sum(krt * krt, axis=0, keepdims=True)
            ss.append(scores(cb, _mm(qr, krt.astype(BF16)), krs))
            cbs.append(cb)
        update(jnp.concatenate(ss, axis=1), jnp.concatenate(cbs, axis=0))

    @pl.when(s_idx == pl.num_programs(1) - 1)
    def _():
        cn = cn_ref[0].astype(BF16)
        krn = krn_ref[0]
        kk = krn * krn
        hi = kk.astype(BF16)
        lo = (kk - hi.astype(F32)).astype(BF16)
        ones = jnp.ones((n_heads, kk.shape[1]), BF16)
        krs = _nt(ones, hi) + _nt(ones, lo)
        qpos = lax.broadcasted_iota(jnp.int32, (rows, cn.shape[0]), 0) % n_q
        tpos = lax.broadcasted_iota(jnp.int32, (rows, cn.shape[0]), 1)
        s = scores(cn, _nt(qp_sc[...].astype(BF16), krn.astype(BF16)), krs)
        update(jnp.where(tpos <= qpos, s, NEG), cn)
        lat = (acc_sc[...] / l_sc[...]).astype(BF16)
        for h in range(n_heads):
            sl = slice(h * HEAD_PAD, (h + 1) * HEAD_PAD)
            o_ref[:, sl] = _mm(lat[h * n_q:(h + 1) * n_q, :], wuv_ref[:, sl])


def _sattn(page_table, q, c_new_pad, kr_new_pad, ckv, krope_t, layer, w, dims, n_pg):
    db, n_pages = page_table.shape
    n_q = q.shape[0] // db
    page, kv_lora = ckv.shape[2], ckv.shape[3]
    d_rope = krope_t.shape[2]
    n_heads = dims["n_heads"]
    hp = n_heads * HEAD_PAD
    steps = n_pages // n_pg
    rows = n_heads * n_q
    n_kn = n_heads * dims["d_nope"]
    pg_chunk = 4 if n_pg % 4 == 0 else 1
    body = functools.partial(_sattn_body, n_pg=n_pg, pg_chunk=pg_chunk, n_heads=n_heads, n_q=n_q,
                             d_nope=dims["d_nope"], d_rope=d_rope, d_head=dims["d_head"])
    pt = page_table.reshape(-1)

    def page_map(p):
        return lambda b, s, pt_ref: (layer, pt_ref[b * n_pages + s * n_pg + p], 0, 0)

    fixed = lambda shape: pl.BlockSpec(shape, lambda b, s, pt_ref: (0,) * len(shape))
    grid_spec = pltpu.PrefetchScalarGridSpec(
        num_scalar_prefetch=1,
        grid=(db, steps),
        in_specs=[pl.BlockSpec((n_q, hp), lambda b, s, pt_ref: (b, 0)),
                  pl.BlockSpec((1, page, kv_lora), lambda b, s, pt_ref: (b, 0, 0)),
                  pl.BlockSpec((1, page, HEAD_PAD), lambda b, s, pt_ref: (b, 0, 0)),
                  fixed(w["wukt"].shape), fixed(w["wukh"].shape), fixed(w["knw"].shape), fixed(w["wuv"].shape)]
                 + [pl.BlockSpec((1, 1, page, kv_lora), page_map(p)) for p in range(n_pg)]
                 + [pl.BlockSpec((1, 1, d_rope, page), page_map(p)) for p in range(n_pg)],
        out_specs=pl.BlockSpec((n_q, hp), lambda b, s, pt_ref: (b, 0)),
        scratch_shapes=[pltpu.VMEM((n_kn + rows, kv_lora), BF16), pltpu.VMEM((rows, HEAD_PAD), F32),
                        pltpu.VMEM((rows, 1), F32), pltpu.VMEM((rows, 1), F32), pltpu.VMEM((rows, kv_lora), F32)],
    )
    return pl.pallas_call(
        body,
        grid_spec=grid_spec,
        out_shape=jax.ShapeDtypeStruct((db * n_q, hp), F32),
        compiler_params=_params(dimension_semantics=("parallel", "arbitrary")),
        name="sattn",
    )(pt, q, c_new_pad, kr_new_pad, w["wukt"], w["wukh"], w["knw"], w["wuv"],
      *([ckv] * n_pg), *([krope_t] * n_pg))


def _post_body(x_ref, a_ref, g_ref, pc_ref, wab_ref, wout_ref, h_ref):
    br = _mm(a_ref[...].astype(BF16), wab_ref[...])
    mixed = g_ref[...].astype(F32) * br + pc_ref[...].astype(F32)
    h_ref[...] = x_ref[...] + _mm(mixed.astype(BF16), wout_ref[...])


def _post(x2d, attn, gates, pc, w, tm):
    n, d = x2d.shape
    row = lambda i: (i, 0)
    return pl.pallas_call(
        _post_body,
        grid=(n // tm,),
        in_specs=[pl.BlockSpec((tm, d), row), pl.BlockSpec((tm, attn.shape[1]), row), pl.BlockSpec((tm, d), row),
                  pl.BlockSpec((tm, d), row), _full(w["wab"].shape), _full(w["wout"].shape)],
        out_specs=pl.BlockSpec((tm, d), row),
        out_shape=jax.ShapeDtypeStruct((n, d), F32),
        compiler_params=_params(dimension_semantics=("parallel",)),
        name="post",
    )(x2d, attn, gates, pc, w["wab"], w["wout"])


def _gelu(x):
    k = -2.0 * math.sqrt(2.0 / math.pi) * math.log2(math.e)
    return x / (1.0 + jnp.exp2(x * (k + (k * 0.044715) * (x * x))))


def _batcher_pairs(n):
    pairs = []
    p = 1
    while p < n:
        k = p
        while k >= 1:
            for j in range(k % p, n - k, 2 * k):
                for i in range(min(k, n - j - k)):
                    if (i + j) // (2 * p) == (i + j + k) // (2 * p):
                        pairs.append((i + j, i + j + k))
            k //= 2
        p *= 2
    return pairs


def _top_merge(s, with_rank=False):
    n = PEER_TOPK
    assert s.shape[0] == 8 * n
    st = [s[8 * v:8 * v + 8] for v in range(n)]
    for a, b in _batcher_pairs(n):
        st[a], st[b] = jnp.maximum(st[a], st[b]), jnp.minimum(st[a], st[b])
    for shift in (4, 2, 1):
        other = [pltpu.roll(x, shift, axis=0) for x in st]
        st = [jnp.maximum(st[i], other[n - 1 - i]) for i in range(n)]
        stride = n // 2
        while stride >= 1:
            for i in range(n):
                if (i // stride) % 2 == 0:
                    j = i + stride
                    st[i], st[j] = jnp.maximum(st[i], st[j]), jnp.minimum(st[i], st[j])
            stride //= 2
    rid = lax.broadcasted_iota(jnp.int32, st[0].shape, 0)
    halves = []
    for half in range(n // 8):
        acc = st[half * 8]
        for a in range(1, 8):
            acc = jnp.where(rid == a, st[half * 8 + a], acc)
        halves.append(acc)
    tops = jnp.concatenate(halves, axis=0)
    if not with_rank:
        return tops
    rank = jnp.zeros(s.shape, F32)
    for b in range(n):
        rank = rank + jnp.where(tops[b:b + 1] > s, 1.0, 0.0)
    return tops, rank


def _top_rows(s, k, with_rank=False):
    r, t = s.shape
    groups = r // 8
    st = [s[8 * v:8 * v + 8] for v in range(groups)]
    for a, b in _batcher_pairs(1 << (groups - 1).bit_length()):
        if b < groups:
            st[a], st[b] = jnp.maximum(st[a], st[b]), jnp.minimum(st[a], st[b])
    rid = lax.broadcasted_iota(jnp.int32, (k, t), 0)
    ninf = jnp.full((1, 8, t), -jnp.inf, F32)

    def step(i, carry):
        stack, tops = carry
        m = jnp.max(stack[0], axis=0, keepdims=True)
        tops = jnp.where(rid == i, m, tops)
        hit = stack[0] == m
        stack = jnp.where(hit[None], jnp.concatenate([stack[1:], ninf], axis=0), stack)
        return stack, tops

    _, tops = lax.fori_loop(0, k, step, (jnp.stack(st), jnp.full((k, t), -jnp.inf, F32)))
    if not with_rank:
        return tops
    rank = jnp.zeros(s.shape, F32)
    for b in range(k):
        rank = rank + jnp.where(tops[b:b + 1] > s, 1.0, 0.0)
    return tops, rank


def _peer_body(h_ref, n2_ref, wqt_ref, keys_ref, u_ref, vt_ref, y_ref,
               hn_sc, qt_sc, r1_sc, e1_sc, b_sc, e0_sc, g_sc, acc_sc,
               *, n_heads, n_keys, irows):
    s_idx = pl.program_id(1)
    k = PEER_TOPK
    t = hn_sc.shape[1]

    @pl.when(s_idx == 0)
    def _():
        hn_sc[...] = _rms(h_ref[...], n2_ref[...]).T.astype(BF16)
        acc_sc[...] = jnp.zeros(acc_sc.shape, F32)
        qrows = qt_sc.shape[0] // 4
        for c in range(4):
            qt_sc[c * qrows:(c + 1) * qrows, :] = _mm(wqt_ref[c * qrows:(c + 1) * qrows, :], hn_sc[...]).astype(BF16)

        def head(h, _):
            def scores(p):
                r0 = pl.multiple_of((h * 2 + p) * n_keys, n_keys)
                return _mm(keys_ref[h * 2 + p], qt_sc[pl.ds(r0, n_keys), :])

            s0 = scores(0)
            s1 = scores(1)
            top0 = _top_merge(s0)
            top1, rank1 = _top_merge(s1, with_rank=True)
            cand = jnp.concatenate(
                [top0[0:1] + top1]
                + [top0[a:a + 1] + top1[0:8] for a in range(1, 8)]
                + [top0[8:16] + top1[0:1]], axis=0)
            ctop = _top_rows(cand, k + 1)
            c_k = ctop[k - 1:k]
            tau = c_k + 0.5 * (ctop[k:k + 1] - c_k)
            m = top0[0:1] + top1[0:1]
            z = jnp.sum(jnp.where(cand >= c_k, jnp.exp(cand - m), 0.0), axis=0, keepdims=True)
            in0 = s0 >= top0[k - 1:k]
            cnt = jnp.zeros(s0.shape, F32)
            for b in range(k):
                cnt = cnt + jnp.where(s0 + top1[b:b + 1] >= tau, 1.0, 0.0)
            b_sc[h] = jnp.where(in0, cnt, 0.0)
            e0_sc[h] = jnp.where(in0, jnp.exp(s0 - top0[0:1]) / z, 0.0)
            e1_sc[h] = jnp.exp(s1 - top1[0:1]).astype(BF16)
            r1_sc[h] = rank1.astype(BF16)
            return 0

        lax.fori_loop(0, n_heads, head, 0)

    base = pl.multiple_of(s_idx * irows, irows)
    for lt in range(t // LANES):
        ls = slice(lt * LANES, (lt + 1) * LANES)
        b_t = [b_sc[h, pl.ds(base, irows), ls] for h in range(n_heads)]
        e0_t = [e0_sc[h, pl.ds(base, irows), ls] for h in range(n_heads)]
        for ii in range(irows):
            wgt = jnp.zeros((n_keys, LANES), BF16)
            for h in range(n_heads):
                sel = r1_sc[h, :, ls] < b_t[h][ii:ii + 1, :].astype(BF16)
                wgt = wgt + jnp.where(sel, e1_sc[h, :, ls] * e0_t[h][ii:ii + 1, :].astype(BF16),
                                      jnp.zeros((), BF16))
            g_sc[ii * n_keys:(ii + 1) * n_keys, ls] = wgt
    act = _gelu(_mm(u_ref[...], hn_sc[...])).astype(BF16)
    acc_sc[...] += _mm(vt_ref[0], g_sc[...] * act)

    @pl.when(s_idx == pl.num_programs(1) - 1)
    def _():
        y_ref[...] = h_ref[...] + acc_sc[...].T


def _peer(h2d, w, tt, et):
    n, d = h2d.shape
    n_heads, n_keys = w["n_peer_heads"], w["n_keys"]
    n_exp = w["pu"].shape[0]
    irows = et // n_keys
    n_blk = n_exp // et
    body = functools.partial(_peer_body, n_heads=n_heads, n_keys=n_keys, irows=irows)
    small32 = pltpu.VMEM((n_heads, n_keys, tt), F32)
    small16 = pltpu.VMEM((n_heads, n_keys, tt), BF16)
    return pl.pallas_call(
        body,
        grid=(n // tt, n_blk),
        in_specs=[pl.BlockSpec((tt, d), lambda i, s: (i, 0)), _full(w["n2"].shape), _full(w["wqt"].shape),
                  _full(w["keys"].shape),
                  pl.BlockSpec((et, d), lambda i, s: (s, 0)),
                  pl.BlockSpec((1, d, et), lambda i, s: (s, 0, 0))],
        out_specs=pl.BlockSpec((tt, d), lambda i, s: (i, 0)),
        out_shape=jax.ShapeDtypeStruct((n, d), F32),
        scratch_shapes=[pltpu.VMEM((d, tt), BF16), pltpu.VMEM((w["wqt"].shape[0], tt), BF16),
                        small16, small16, small32, small32,
                        pltpu.VMEM((et, tt), BF16), pltpu.VMEM((d, tt), F32)],
        compiler_params=_params(dimension_semantics=("parallel", "arbitrary")),
        name="peer",
    )(h2d, w["n2"], w["wqt"], w["keys"], w["pu"], w["pvt"])


def _rot_cols(wr):
    half = wr.shape[-1] // 2
    return jnp.concatenate([-wr[..., half:], wr[..., :half]], axis=-1)


def _rope_tables(pos, d_nope, d_rope):
    half = d_rope // 2
    inv_freq = ROPE_THETA ** (-jnp.arange(half, dtype=F32) / half)
    ang = pos.astype(F32)[:, None] * inv_freq[None, :]
    cos = jnp.cos(ang)
    sin = jnp.sin(ang)
    n = pos.shape[0]
    pad = jnp.zeros((n, HEAD_PAD - d_nope - d_rope), F32)
    ta = jnp.concatenate([jnp.ones((n, d_nope), F32), cos, cos, pad], axis=-1)
    tb = jnp.concatenate([jnp.zeros((n, d_nope), F32), sin, sin, pad], axis=-1)
    return ta, tb


def _layer_weights(l, norm1_w, w_in, q_lat_norm_w, w_q_up, kv_lat_norm_w, w_uk, w_uv, q_norm_w, k_norm_w,
                   w_attn_br, pool_lin_w, pool_scale, w_pool_br, w_out, norm2_w, peer_w_q, peer_keys,
                   peer_u, peer_v, d_rope):
    d = w_in.shape[1]
    q_lora = w_q_up.shape[1]
    n_heads, d_head = w_q_up.shape[2], w_q_up.shape[3]
    kv_lora = w_uk.shape[1]
    d_nope = w_uk.shape[3]
    d_v = w_uv.shape[3]
    off_kv = q_lora
    off_kr = off_kv + kv_lora
    off_pool = off_kr + d_rope
    scale = d_head ** -0.5
    assert d_head == d_nope + d_rope and d_nope + 2 * d_rope == HEAD_PAD and d_v < HEAD_PAD
    assert peer_keys.shape[3] == 8 * PEER_TOPK and peer_u.shape[1] == peer_keys.shape[3] ** 2
    wi = w_in[l]
    w_kr = wi[:, off_kr:off_pool]
    krblk = jnp.concatenate([jnp.zeros((d, d_nope), F32), w_kr, _rot_cols(w_kr)], axis=-1)
    win = jnp.concatenate([wi[:, :off_kr], krblk, wi[:, off_pool:]], axis=-1).astype(BF16)
    wq = w_q_up[l]
    wq = jnp.concatenate([wq[..., :d_nope], wq[..., d_nope:], _rot_cols(wq[..., d_nope:])], axis=-1)
    wq = wq.reshape(q_lora, n_heads * HEAD_PAD).astype(BF16)
    padk = lambda a, width: jnp.concatenate([a, jnp.zeros(a.shape[:-1] + (HEAD_PAD - width,), F32)], axis=-1)
    wuk_pad = padk(w_uk[l], d_nope)
    wuv_pad = padk(w_uv[l], d_v)
    wab = jnp.concatenate([w_attn_br[l].reshape(n_heads, d_v, d),
                           jnp.zeros((n_heads, HEAD_PAD - d_v, d), F32)], axis=1).reshape(n_heads * HEAD_PAD, d)
    n_ph, _, n_keys, half = peer_keys.shape[1:]
    return dict(
        n1=norm1_w[l][None, :], win=win, qlw=q_lat_norm_w[l][None, :], wq=wq, kvw=kv_lat_norm_w[l][None, :],
        wuk=wuk_pad.reshape(kv_lora, n_heads * HEAD_PAD).astype(BF16),
        wuv=wuv_pad.reshape(kv_lora, n_heads * HEAD_PAD).astype(BF16),
        wuvt=wuv_pad.reshape(kv_lora, n_heads * HEAD_PAD).T.astype(BF16),
        qnw=padk(q_norm_w[l], d_head)[None, :] * scale, knw=padk(k_norm_w[l], d_head)[None, :],
        wukt=jnp.transpose(w_uk[l], (2, 1, 0)).reshape(d_nope * n_heads, kv_lora).astype(BF16),
        wukh=jnp.transpose(wuk_pad, (1, 0, 2)).astype(BF16),
        wab=wab.astype(BF16), wout=w_out[l].astype(BF16),
        pool_lin=pool_lin_w[l].astype(BF16), pool_scale=pool_scale[l][None, :], wpbr=w_pool_br[l].astype(BF16),
        n2=norm2_w[l][None, :], wqt=peer_w_q[l].T.astype(BF16),
        keys=peer_keys[l].reshape(n_ph * 2, n_keys, half).astype(BF16),
        pu=peer_u[l].astype(BF16),
        n_peer_heads=n_ph, n_keys=n_keys,
    ), dict(q_lora=q_lora, kv_lora=kv_lora, pool_dim=wi.shape[1] - off_pool - 2 * d, n_heads=n_heads,
            d_head=d_head, d_nope=d_nope, d_v=d_v, d_rope=d_rope)


def _tile(n, pref):
    t = pref
    while n % t:
        t //= 2
    return t


def kernel(x_prompt, x_sample, cache_ckv, cache_krope, state_pool, page_table, norm1_w, w_in, q_lat_norm_w, w_q_up, kv_lat_norm_w, w_uk, w_uv, q_norm_w, k_norm_w, w_attn_br, pool_lin_w, pool_scale, w_pool_br, w_out, norm2_w, peer_w_q, peer_keys, peer_u, peer_v):
    batch, seq, d = x_prompt.shape
    db, n_q, _ = x_sample.shape
    depth = w_in.shape[0]
    page = cache_ckv.shape[2]
    d_rope = cache_krope.shape[3]
    n_pages = page_table.shape[1]
    past_len = n_pages * page
    n_state = state_pool.shape[2]
    n_p, n_s = batch * seq, db * n_q
    xp = x_prompt.reshape(n_p, d)
    xs = x_sample.reshape(n_s, d)
    outs = [[] for _ in range(6)]
    for l in range(depth):
        w, dims = _layer_weights(l, norm1_w, w_in, q_lat_norm_w, w_q_up, kv_lat_norm_w, w_uk, w_uv, q_norm_w,
                                 k_norm_w, w_attn_br, pool_lin_w, pool_scale, w_pool_br, w_out, norm2_w,
                                 peer_w_q, peer_keys, peer_u, peer_v, d_rope)
        d_nope = dims["d_nope"]
        kv_lora = dims["kv_lora"]
        pool_dim = dims["pool_dim"]
        rope_sl = slice(d_nope, d_nope + d_rope)
        tm_p = _tile(seq, 512)
        tm_s = _tile(n_s, 256)
        tt_p = _tile(n_p, 512)
        tt_s = _tile(n_s, 512)
        et = _tile(w["pu"].shape[0], 2048)
        w["pvt"] = jnp.swapaxes(peer_v[l].reshape(-1, et, d), 1, 2).astype(BF16)

        ta, tb = _rope_tables(jnp.arange(seq), d_nope, d_rope)
        q, k, v, c, krp, u, gates = _proj(xp, ta, tb, w, dims, tm_p)
        attn = _flash(q, k, v, batch, seq, dims["n_heads"], _tile(seq, 512), _tile(dims["n_heads"], 8),
                      dims["d_v"])
        pc = _pool_prompt(u, gates, w, batch, seq, tm_p)
        h_p = _post(xp, attn, gates, pc, w, tt_p)
        outs[0].append(c.reshape(batch, seq, kv_lora))
        outs[1].append(krp[:, rope_sl].reshape(batch, seq, d_rope))
        outs[2].append(u.reshape(batch, seq, pool_dim)[:, seq - n_state:])

        pos_s = past_len + jnp.arange(n_q)
        ta, tb = _rope_tables(jnp.tile(pos_s, tm_s // n_q), d_nope, d_rope)
        q, _, _, c, krp, u, gates = _proj(xs, ta, tb, w, dims, tm_s)
        zrow = lambda a, rows: jnp.concatenate(
            [a, jnp.zeros((a.shape[0], rows - a.shape[1], a.shape[2]), a.dtype)], axis=1)
        c_new = zrow(c.reshape(db, n_q, kv_lora), page)
        kr_new = zrow(krp.reshape(db, n_q, HEAD_PAD), page)
        n_pg = _tile(n_pages, 128)
        krope_t = jnp.swapaxes(cache_krope, 2, 3)
        attn = _sattn(page_table, q.astype(F32), c_new, kr_new, cache_ckv, krope_t, l, w, dims, n_pg)
        u3 = u.reshape(db, n_q, pool_dim)
        u_full = jnp.concatenate([state_pool[l], u3], axis=1)
        rows = -(-(n_state + n_q + 1) // 8) * 8
        ext = jnp.concatenate([jnp.zeros((db, rows - n_state - n_q, pool_dim), F32), u_full], axis=1)
        pc = _pool_sample(ext.reshape(db * rows, pool_dim), gates, w, db, rows, n_q)
        h_s = _post(xs, attn, gates, pc, w, tt_s)
        xp = _peer(h_p, w, tt_p, et)
        xs = _peer(h_s, w, tt_s, et)
        outs[3].append(c.reshape(db, n_q, kv_lora))
        outs[4].append(krp[:, rope_sl].reshape(db, n_q, d_rope))
        outs[5].append(u_full[:, -n_state:])
    return (xp.reshape(batch, seq, d), xs.reshape(db, n_q, d)) + tuple(jnp.stack(o) for o in outs)
```
